```python
import math
import jax, jax.numpy as jnp
from jax import lax
import numpy as np

D_MODEL = 1024
BATCH = 2
SEQ = 16384
DEPTH = 1
DEC_BATCH = 16
DEC_SEQ = 32
PAST_LEN = 1024

CHUNK = 64
N_META = 16
FFN_DIM = ((8 * D_MODEL // 3 + 127) // 128) * 128
NORM_EPS = 1e-6
D_INNER = 2 * D_MODEL
SSM_HEAD_DIM = 64
SSM_HEADS = D_INNER // SSM_HEAD_DIM
SSM_GROUPS = 4
SSM_STATE = 128
CONV_WIDTH = 4
CONV_DIM = D_INNER + 2 * SSM_GROUPS * SSM_STATE
SSD_BLOCK = 64
MLA_HEADS = D_MODEL // 64
Q_LORA = 3 * D_MODEL // 8
KV_LORA = D_MODEL // 4
QK_NOPE_DIM = 64
QK_ROPE_DIM = 32
V_HEAD_DIM = 64
ROPE_THETA = 10000.0
Q_BLOCK = 128
OFF_Z = 0
OFF_XBC = OFF_Z + D_INNER
OFF_DT = OFF_XBC + CONV_DIM
OFF_CQ = OFF_DT + SSM_HEADS
OFF_CKV = OFF_CQ + Q_LORA
OFF_KROPE = OFF_CKV + KV_LORA
OFF_GATE = OFF_KROPE + QK_ROPE_DIM
IN_DIM = OFF_GATE + 2 * D_MODEL

kernel_name = 'hybrid_ssd_mla_streaming_encoder_step'


def rmsnorm(x, g):
    xf = x.astype(jnp.float32)
    y = xf * lax.rsqrt(jnp.mean(xf * xf, axis=-1, keepdims=True) + NORM_EPS)
    return (y * g.astype(jnp.float32)).astype(x.dtype)


def swiglu(xn, w_gu, w_down):
    gu = xn @ w_gu
    return (jax.nn.silu(gu[..., :FFN_DIM]) * gu[..., FFN_DIM:]) @ w_down


def rope(x, pos):
    half = x.shape[-1] // 2
    inv = ROPE_THETA ** (-jnp.arange(half, dtype=jnp.float32) / half)
    ang = pos.astype(jnp.float32)[:, None] * inv[None, :]
    shape = (1, pos.shape[0]) + (1,) * (x.ndim - 3) + (half,)
    cos = jnp.cos(ang).reshape(shape)
    sin = jnp.sin(ang).reshape(shape)
    xf = x.astype(jnp.float32)
    x1, x2 = xf[..., :half], xf[..., half:]
    return jnp.concatenate([x1 * cos - x2 * sin, x1 * sin + x2 * cos], axis=-1).astype(x.dtype)


def ssd_scan(x, dt, A, B, C, state0):
    b, t, nh, hp = x.shape
    ng, ns = B.shape[2], B.shape[3]
    hg = nh // ng
    pad = (-t) % SSD_BLOCK
    nc = (t + pad) // SSD_BLOCK

    def blocks(a, tail):
        a = jnp.pad(a.astype(jnp.float32), [(0, 0), (0, pad)] + [(0, 0)] * (a.ndim - 2))
        return jnp.moveaxis(a.reshape((b, nc, SSD_BLOCK) + tail), 1, 0)

    xc = blocks(x.reshape(b, t, ng, hg, hp), (ng, hg, hp))
    dtc = blocks(dt.reshape(b, t, ng, hg), (ng, hg))
    bc = blocks(B, (ng, ns))
    cc = blocks(C, (ng, ns))
    a_g = A.astype(jnp.float32).reshape(ng, hg)
    causal = jnp.tril(jnp.ones((SSD_BLOCK, SSD_BLOCK), dtype=bool))

    def step(state, inp):
        xk, dtk, bk, ck = inp
        a_cs = jnp.cumsum(dtk * a_g, axis=1)
        seg = a_cs[:, :, None] - a_cs[:, None, :]
        decay = jnp.exp(jnp.where(causal[None, :, :, None, None], seg, -jnp.inf))
        cb = jnp.einsum('bign,bjgn->bijg', ck, bk)
        xdt = dtk[..., None] * xk
        y = jnp.einsum('bijg,bijgh,bjghp->bighp', cb, decay, xdt)
        y = y + jnp.einsum('bign,bghpn->bighp', ck, state) * jnp.exp(a_cs)[..., None]
        total = a_cs[:, -1]
        w = jnp.exp(total[:, None] - a_cs)[..., None] * xdt
        state = state * jnp.exp(total)[..., None, None] + jnp.einsum('bjgn,bjghp->bghpn', bk, w)
        return state, y

    s0 = state0.astype(jnp.float32).reshape(b, ng, hg, hp, ns)
    s_t, ys = lax.scan(step, s0, (xc, dtc, bc, cc))
    y = jnp.moveaxis(ys, 0, 1).reshape(b, nc * SSD_BLOCK, nh, hp)[:, :t]
    return y, s_t.reshape(b, nh, hp, ns)


def ssd_branch(z, xbc, dt_raw, conv_prev, ssm_prev, lp):
    b, t = xbc.shape[:2]
    if conv_prev is None:
        conv_prev = jnp.zeros((b, CONV_WIDTH - 1, CONV_DIM), xbc.dtype)
    if ssm_prev is None:
        ssm_prev = jnp.zeros((b, SSM_HEADS, SSM_HEAD_DIM, SSM_STATE), jnp.float32)
    full = jnp.concatenate([conv_prev.astype(xbc.dtype), xbc], axis=1)
    conv_w = lp['conv_w']
    conv = lp['conv_b'] + full[:, 0:t] * conv_w[0]
    for k in range(1, CONV_WIDTH):
        conv = conv + full[:, k:k + t] * conv_w[k]
    xbc_c = jax.nn.silu(conv)
    xs = xbc_c[..., :D_INNER].reshape(b, t, SSM_HEADS, SSM_HEAD_DIM)
    bm = xbc_c[..., D_INNER:D_INNER + SSM_GROUPS * SSM_STATE].reshape(b, t, SSM_GROUPS, SSM_STATE)
    cm = xbc_c[..., D_INNER + SSM_GROUPS * SSM_STATE:].reshape(b, t, SSM_GROUPS, SSM_STATE)
    dt = jax.nn.softplus(dt_raw.astype(jnp.float32) + lp['dt_bias'].astype(jnp.float32))
    a = -jnp.exp(lp['a_log'].astype(jnp.float32))
    y, ssm_new = ssd_scan(xs, dt, a, bm, cm, ssm_prev)
    y = y + lp['ssm_d'].astype(jnp.float32)[:, None] * xs.astype(jnp.float32)
    y = y.reshape(b, t, D_INNER) * jax.nn.silu(z.astype(jnp.float32))
    yg = y.reshape(b, t, SSM_GROUPS, D_INNER // SSM_GROUPS)
    yg = yg * lax.rsqrt(jnp.mean(yg * yg, axis=-1, keepdims=True) + NORM_EPS)
    y = (yg.reshape(b, t, D_INNER) * lp['ssm_norm'].astype(jnp.float32)).astype(z.dtype)
    return y @ lp['ssm_w_out'], full[:, -(CONV_WIDTH - 1):], ssm_new


def chunk_causal_attention(q_nope, q_rope, k_nope, k_rope, v, q_chunk, k_chunk):
    b, t = q_nope.shape[:2]
    scale = (QK_NOPE_DIM + QK_ROPE_DIM) ** -0.5

    def attend(blk):
        qn, qr, qc = blk
        s = jnp.einsum('bqhd,bkhd->bhqk', qn, k_nope) + jnp.einsum('bqhd,bkd->bhqk', qr, k_rope)
        visible = k_chunk[None, :] <= qc[:, None]
        s = jnp.where(visible[None, None], s.astype(jnp.float32) * scale, -jnp.inf)
        p = jax.nn.softmax(s, axis=-1).astype(v.dtype)
        return jnp.einsum('bhqk,bkhd->bqhd', p, v)

    if t <= Q_BLOCK:
        return attend((q_nope, q_rope, q_chunk))
    pad = (-t) % Q_BLOCK
    nb = (t + pad) // Q_BLOCK

    def blocks(a):
        a = jnp.pad(a, [(0, 0), (0, pad)] + [(0, 0)] * (a.ndim - 2))
        return jnp.moveaxis(a.reshape((b, nb, Q_BLOCK) + a.shape[2:]), 1, 0)

    qc = jnp.pad(q_chunk, (0, pad), mode='edge').reshape(nb, Q_BLOCK)
    out = lax.map(attend, (blocks(q_nope), blocks(q_rope), qc))
    out = jnp.moveaxis(out, 0, 1).reshape(b, nb * Q_BLOCK, MLA_HEADS, V_HEAD_DIM)
    return out[:, :t]


def mla_branch(cq, q_pos, k_ckv, k_rope, k_pos, lp):
    b, t = cq.shape[:2]
    nk = k_ckv.shape[1]
    q = (rmsnorm(cq, lp['q_norm']) @ lp['w_uq']).reshape(b, t, MLA_HEADS, QK_NOPE_DIM + QK_ROPE_DIM)
    q_nope = q[..., :QK_NOPE_DIM]
    q_rope = rope(q[..., QK_NOPE_DIM:], q_pos)
    kv = (k_ckv @ lp['w_ukv']).reshape(b, nk, MLA_HEADS, QK_NOPE_DIM + V_HEAD_DIM)
    o = chunk_causal_attention(q_nope, q_rope, kv[..., :QK_NOPE_DIM], k_rope, kv[..., QK_NOPE_DIM:],
                               q_pos // CHUNK, k_pos // CHUNK)
    return o.reshape(b, t, MLA_HEADS * V_HEAD_DIM) @ lp['mla_w_out']


def pre_mix(x, lp):
    h = x + 0.5 * swiglu(rmsnorm(x, lp['ffn1_norm']), lp['ffn1_w_gu'], lp['ffn1_w_down'])
    return h, rmsnorm(h, lp['mix_norm'])


def kv_rows(kvp, pos, lp):
    return rmsnorm(kvp[..., :KV_LORA], lp['kv_norm']), rope(kvp[..., KV_LORA:], pos)


def run_layer(x, pos, prefix, conv_prev, ssm_prev, lp):
    h, u = pre_mix(x, lp)
    proj = u @ lp['w_in']
    y_s, conv_new, ssm_new = ssd_branch(proj[..., OFF_Z:OFF_XBC], proj[..., OFF_XBC:OFF_DT],
                                        proj[..., OFF_DT:OFF_CQ], conv_prev, ssm_prev, lp)
    ckv, krope = kv_rows(proj[..., OFF_CKV:OFF_GATE], pos, lp)
    if prefix is None:
        keys_ckv, keys_kr, k_pos = ckv, krope, pos
    else:
        p_ckv, p_kr, p_pos = prefix
        keys_ckv = jnp.concatenate([p_ckv.astype(ckv.dtype), ckv], axis=1)
        keys_kr = jnp.concatenate([p_kr.astype(krope.dtype), krope], axis=1)
        k_pos = jnp.concatenate([p_pos, pos])
    y_m = mla_branch(proj[..., OFF_CQ:OFF_CKV], pos, keys_ckv, keys_kr, k_pos, lp)
    gates = jax.nn.sigmoid((proj[..., OFF_GATE:] + lp['gate_bias']).astype(jnp.float32)).astype(x.dtype)
    merged = gates[..., :D_MODEL] * y_s + gates[..., D_MODEL:] * y_m
    h = h + merged @ lp['w_mix_out']
    h = h + 0.5 * swiglu(rmsnorm(h, lp['ffn2_norm']), lp['ffn2_w_gu'], lp['ffn2_w_down'])
    return h, (ckv, krope, ssm_new, conv_new)


def setup_inputs(seed: int = 0) -> dict:
    key = jax.random.key(seed)
    ks = jax.random.split(key, 32)
    f32 = jnp.float32
    L = DEPTH

    def nrm(k, shape, scale):
        return jax.random.normal(k, shape, f32) * scale

    def gain(k, shape):
        return 1.0 + 0.02 * jax.random.normal(k, shape, f32)

    dt0 = jnp.exp(jax.random.uniform(ks[15], (L, SSM_HEADS), f32, math.log(1e-3), math.log(1e-1)))
    return {
        'x_prompt': nrm(ks[0], (BATCH, SEQ, D_MODEL), 1.0),
        'x_sample': nrm(ks[1], (DEC_BATCH, DEC_SEQ, D_MODEL), 1.0),
        'cache_ckv': nrm(ks[2], (L, DEC_BATCH, PAST_LEN, KV_LORA), 1.0),
        'cache_krope': nrm(ks[3], (L, DEC_BATCH, PAST_LEN, QK_ROPE_DIM), 1.0),
        'state_ssm': nrm(ks[4], (L, DEC_BATCH, SSM_HEADS, SSM_HEAD_DIM, SSM_STATE), 0.5),
        'state_conv': nrm(ks[5], (L, DEC_BATCH, CONV_WIDTH - 1, CONV_DIM), 1.0),
        'meta_tokens': nrm(ks[6], (N_META, D_MODEL), 1.0),
        'ffn1_norm': gain(ks[7], (L, D_MODEL)),
        'ffn1_w_gu': nrm(ks[8], (L, D_MODEL, 2 * FFN_DIM), D_MODEL ** -0.5),
        'ffn1_w_down': nrm(ks[9], (L, FFN_DIM, D_MODEL), FFN_DIM ** -0.5),
        'mix_norm': gain(ks[10], (L, D_MODEL)),
        'w_in': nrm(ks[11], (L, D_MODEL, IN_DIM), D_MODEL ** -0.5),
        'gate_bias': nrm(ks[12], (L, 2 * D_MODEL), 0.1),
        'conv_w': nrm(ks[13], (L, CONV_WIDTH, CONV_DIM), CONV_WIDTH ** -0.5),
        'conv_b': nrm(ks[14], (L, CONV_DIM), 0.02),
        'dt_bias': dt0 + jnp.log(-jnp.expm1(-dt0)),
        'a_log': jnp.log(jax.random.uniform(ks[16], (L, SSM_HEADS), f32, 1.0, 16.0)),
        'ssm_d': 1.0 + 0.1 * jax.random.normal(ks[17], (L, SSM_HEADS), f32),
        'ssm_norm': gain(ks[18], (L, D_INNER)),
        'ssm_w_out': nrm(ks[19], (L, D_INNER, D_MODEL), D_INNER ** -0.5),
        'q_norm': gain(ks[20], (L, Q_LORA)),
        'w_uq': nrm(ks[21], (L, Q_LORA, MLA_HEADS * (QK_NOPE_DIM + QK_ROPE_DIM)), Q_LORA ** -0.5),
        'kv_norm': gain(ks[22], (L, KV_LORA)),
        'w_ukv': nrm(ks[23], (L, KV_LORA, MLA_HEADS * (QK_NOPE_DIM + V_HEAD_DIM)), KV_LORA ** -0.5),
        'mla_w_out': nrm(ks[24], (L, MLA_HEADS * V_HEAD_DIM, D_MODEL), (MLA_HEADS * V_HEAD_DIM) ** -0.5),
        'w_mix_out': nrm(ks[25], (L, D_MODEL, D_MODEL), D_MODEL ** -0.5),
        'ffn2_norm': gain(ks[26], (L, D_MODEL)),
        'ffn2_w_gu': nrm(ks[27], (L, D_MODEL, 2 * FFN_DIM), D_MODEL ** -0.5),
        'ffn2_w_down': nrm(ks[28], (L, FFN_DIM, D_MODEL), FFN_DIM ** -0.5),
        'final_norm': gain(ks[29], (D_MODEL,)),
    }


def reference(x_prompt, x_sample, cache_ckv, cache_krope, state_ssm, state_conv, meta_tokens,
              ffn1_norm, ffn1_w_gu, ffn1_w_down, mix_norm, w_in, gate_bias, conv_w, conv_b,
              dt_bias, a_log, ssm_d, ssm_norm, ssm_w_out, q_norm, w_uq, kv_norm, w_ukv, mla_w_out,
              w_mix_out, ffn2_norm, ffn2_w_gu, ffn2_w_down, final_norm):
    b, seq = x_prompt.shape[:2]
    db, dseq = x_sample.shape[:2]
    past = cache_ckv.shape[2]
    meta = meta_tokens.astype(x_prompt.dtype)[None]
    meta_pos = jnp.arange(N_META, dtype=jnp.int32) - N_META
    p_pos = jnp.arange(N_META + seq, dtype=jnp.int32) - N_META
    s_pos = past + jnp.arange(dseq, dtype=jnp.int32)
    prefix_pos = jnp.concatenate([meta_pos, jnp.arange(past, dtype=jnp.int32)])

    hp = jnp.concatenate([jnp.broadcast_to(meta, (b, N_META, D_MODEL)), x_prompt], axis=1)
    hs = x_sample
    hm = meta
    ckv_p, kr_p, ssm_p, conv_p = [], [], [], []
    ckv_s, kr_s, ssm_s, conv_s = [], [], [], []
    for l in range(DEPTH):
        lp = {'ffn1_norm': ffn1_norm[l], 'ffn1_w_gu': ffn1_w_gu[l], 'ffn1_w_down': ffn1_w_down[l],
              'mix_norm': mix_norm[l], 'w_in': w_in[l], 'gate_bias': gate_bias[l],
              'conv_w': conv_w[l], 'conv_b': conv_b[l], 'dt_bias': dt_bias[l], 'a_log': a_log[l],
              'ssm_d': ssm_d[l], 'ssm_norm': ssm_norm[l], 'ssm_w_out': ssm_w_out[l],
              'q_norm': q_norm[l], 'w_uq': w_uq[l], 'kv_norm': kv_norm[l], 'w_ukv': w_ukv[l],
              'mla_w_out': mla_w_out[l], 'w_mix_out': w_mix_out[l], 'ffn2_norm': ffn2_norm[l],
              'ffn2_w_gu': ffn2_w_gu[l], 'ffn2_w_down': ffn2_w_down[l]}
        hp, st = run_layer(hp, p_pos, None, None, None, lp)
        ckv_p.append(st[0]); kr_p.append(st[1]); ssm_p.append(st[2]); conv_p.append(st[3])
        _, um = pre_mix(hm, lp)
        m_ckv, m_kr = kv_rows(um @ lp['w_in'][:, OFF_CKV:OFF_GATE], meta_pos, lp)
        prefix = (jnp.concatenate([jnp.broadcast_to(m_ckv, (db, N_META, KV_LORA)), cache_ckv[l].astype(m_ckv.dtype)], axis=1),
                  jnp.concatenate([jnp.broadcast_to(m_kr, (db, N_META, QK_ROPE_DIM)), cache_krope[l].astype(m_kr.dtype)], axis=1),
                  prefix_pos)
        hs, st = run_layer(hs, s_pos, prefix, state_conv[l], state_ssm[l], lp)
        ckv_s.append(st[0]); kr_s.append(st[1]); ssm_s.append(st[2]); conv_s.append(st[3])
        if l + 1 < DEPTH:
            hm = run_layer(hm, meta_pos, None, None, None, lp)[0]

    y_prompt = rmsnorm(hp[:, N_META:], final_norm)
    y_sample = rmsnorm(hs, final_norm)
    return (y_prompt, y_sample,
            jnp.stack(ckv_p), jnp.stack(kr_p), jnp.stack(ssm_p), jnp.stack(conv_p),
            jnp.stack(ckv_s), jnp.stack(kr_s), jnp.stack(ssm_s), jnp.stack(conv_s))
```

```python
import functools
import math

import jax
import jax.numpy as jnp
from jax import lax
from jax.experimental import pallas as pl
from jax.experimental.pallas import tpu as pltpu

F32 = jnp.float32
BF16 = jnp.bfloat16

D_MODEL = 1024
CHUNK = 64
FFN_DIM = 2816
NORM_EPS = 1e-6
D_INNER = 2048
SSM_HEAD_DIM = 64
SSM_HEADS = 32
SSM_GROUPS = 4
SSM_STATE = 128
CONV_WIDTH = 4
CONV_DIM = D_INNER + 2 * SSM_GROUPS * SSM_STATE
MLA_HEADS = 16
Q_LORA = 384
KV_LORA = 256
QK_NOPE_DIM = 64
QK_ROPE_DIM = 32
V_HEAD_DIM = 64
ROPE_THETA = 10000.0
OFF_Z = 0
OFF_XBC = OFF_Z + D_INNER
OFF_DT = OFF_XBC + CONV_DIM
OFF_CQ = OFF_DT + SSM_HEADS
OFF_CKV = OFF_CQ + Q_LORA
OFF_KROPE = OFF_CKV + KV_LORA
OFF_GATE = OFF_KROPE + QK_ROPE_DIM

LANES = 128
SSD_L = 128
FFN_TF = 256
NEG_BIG = -1e30
VMEM_LIMIT = 52 * 1024 * 1024

MISC_CQ = 0
MISC_CKV = 384
MISC_KRA = 640
MISC_KRB = 768
MISC_DT = 896
MISC_W = 1024


def _cparams(n_grid):
    return pltpu.CompilerParams(dimension_semantics=("arbitrary",) * n_grid,
                                vmem_limit_bytes=VMEM_LIMIT)


def _row_tile(m, pref):
    if m <= pref:
        return m
    t = pref
    while t >= 16:
        if m % t == 0:
            return t
        t //= 2
    return m


def _rms(x, g):
    return x * lax.rsqrt(jnp.mean(x * x, axis=-1, keepdims=True) + NORM_EPS) * g


def _dot(a, b):
    return jnp.dot(a, b, preferred_element_type=F32)


def _ffn_body(x_ref, g1_ref, wg_ref, wu_ref, wd_ref, g2_ref, *rest, emit_h):
    if emit_h:
        h_ref, n_ref, xn_scr, acc_scr = rest
    else:
        n_ref, xn_scr, acc_scr = rest
    f = pl.program_id(1)

    @pl.when(f == 0)
    def _():
        xn_scr[...] = _rms(x_ref[...], g1_ref[...]).astype(BF16)
        acc_scr[...] = jnp.zeros_like(acc_scr)

    xn = xn_scr[...]
    g = _dot(xn, wg_ref[...])
    u = _dot(xn, wu_ref[...])
    act = (g * jax.nn.sigmoid(g) * u).astype(BF16)
    acc_scr[...] += _dot(act, wd_ref[...])

    @pl.when(f == pl.num_programs(1) - 1)
    def _():
        h = x_ref[...] + 0.5 * acc_scr[...]
        if emit_h:
            h_ref[...] = h
        n_ref[...] = _rms(h, g2_ref[...]).astype(n_ref.dtype)


def _ffn(x, g1, w_gu, w_down, g2, *, emit_h, out_dtype, tm_pref=1024):
    m = x.shape[0]
    tm = _row_tile(m, tm_pref)
    nf = FFN_DIM // FFN_TF
    row = pl.BlockSpec((tm, D_MODEL), lambda i, f: (i, 0))
    vec = pl.BlockSpec((1, D_MODEL), lambda i, f: (0, 0))
    out_shape = [jax.ShapeDtypeStruct((m, D_MODEL), out_dtype)]
    out_specs = [row]
    if emit_h:
        out_shape = [jax.ShapeDtypeStruct((m, D_MODEL), F32)] + out_shape
        out_specs = [row, row]
    return pl.pallas_call(
        functools.partial(_ffn_body, emit_h=emit_h),
        out_shape=out_shape,
        grid=(m // tm, nf),
        in_specs=[row, vec,
                  pl.BlockSpec((D_MODEL, FFN_TF), lambda i, f: (0, f)),
                  pl.BlockSpec((D_MODEL, FFN_TF), lambda i, f: (0, f + nf)),
                  pl.BlockSpec((FFN_TF, D_MODEL), lambda i, f: (f, 0)),
                  vec],
        out_specs=out_specs,
        scratch_shapes=[pltpu.VMEM((tm, D_MODEL), BF16), pltpu.VMEM((tm, D_MODEL), F32)],
        compiler_params=_cparams(2),
        name="ffn",
    )(x, g1.reshape(1, -1), w_gu, w_gu, w_down, g2.reshape(1, -1))


def _mm_body(x_ref, w_ref, o_ref):
    o_ref[...] = _dot(x_ref[...], w_ref[...]).astype(o_ref.dtype)


def _matmul(x, w, out_dtype, *, tm_pref=1024, tn=512):
    m, k = x.shape
    n = w.shape[1]
    tm = _row_tile(m, tm_pref)
    return pl.pallas_call(
        _mm_body,
        out_shape=jax.ShapeDtypeStruct((m, n), out_dtype),
        grid=(m // tm, n // tn),
        in_specs=[pl.BlockSpec((tm, k), lambda i, j: (i, 0)),
                  pl.BlockSpec((k, tn), lambda i, j: (0, j))],
        out_specs=pl.BlockSpec((tm, tn), lambda i, j: (i, j)),
        compiler_params=_cparams(2),
        name="proj",
    )(x, w)


def _softplus(x):
    return jnp.maximum(x, 0.0) + jnp.log1p(jnp.exp(-jnp.abs(x)))


def _expand_heads(x, e):
    hi = x.astype(BF16)
    lo = (x - hi.astype(F32)).astype(BF16)
    return _dot(hi, e) + _dot(lo, e)


def _ssd_body(xbc_ref, dtb_ref, z_ref, convp_ref, ssmp_ref, cw_ref, cb_ref, dtbias_ref, alog_ref,
              dvec_ref, nrm_ref, e_ref, y_ref, ssm_ref, xpad, state, y_scr, *, t_valid):
    t = pl.program_id(1)
    L = SSD_L
    gs = SSM_STATE
    hg = SSM_HEADS // SSM_GROUPS
    gw = hg * SSM_HEAD_DIM

    @pl.when(t == 0)
    def _():
        xpad[0:8, :] = convp_ref[...]
        state[...] = ssmp_ref[...].T

    xpad[8:8 + L, :] = xbc_ref[...]
    conv = cb_ref[...] + xpad[5:5 + L, :] * cw_ref[0:1, :]
    conv = conv + xpad[6:6 + L, :] * cw_ref[1:2, :]
    conv = conv + xpad[7:7 + L, :] * cw_ref[2:3, :]
    conv = conv + xpad[8:8 + L, :] * cw_ref[3:4, :]
    xpad[0:8, :] = xpad[L:L + 8, :]
    xc = conv * jax.nn.sigmoid(conv)
    xs = xc[:, :D_INNER]
    bm = xc[:, D_INNER:D_INNER + SSM_GROUPS * gs].astype(BF16)
    cm = xc[:, D_INNER + SSM_GROUPS * gs:].astype(BF16)

    dt = _softplus(dtb_ref[:, 0:SSM_HEADS] + dtbias_ref[...])
    if t_valid is not None:
        rows = t * L + lax.broadcasted_iota(jnp.int32, (L, SSM_HEADS), 0)
        dt = jnp.where(rows < t_valid, dt, 0.0)
    da = dt * (-jnp.exp(alog_ref[...]))
    ri = lax.broadcasted_iota(jnp.int32, (L, L), 0)
    ci = lax.broadcasted_iota(jnp.int32, (L, L), 1)
    causal = ri >= ci
    a_cs = jnp.dot(causal.astype(F32), da, precision=lax.Precision.HIGHEST,
                   preferred_element_type=F32)
    eye = (lax.broadcasted_iota(jnp.int32, (SSM_HEADS, SSM_HEADS), 0)
           == lax.broadcasted_iota(jnp.int32, (SSM_HEADS, SSM_HEADS), 1)).astype(F32)
    a_cs_t = lax.dot_general(eye, a_cs, (((1,), (1,)), ((), ())), precision=lax.Precision.HIGHEST,
                             preferred_element_type=F32)
    total = a_cs[L - 1:L, :]

    e = e_ref[...]
    dt_e = _expand_heads(dt, e)
    ea_e = _expand_heads(jnp.exp(a_cs), e)
    wd_e = _expand_heads(jnp.exp(total - a_cs) * dt, e)
    dec_e = _expand_heads(jnp.broadcast_to(jnp.exp(total), (8, SSM_HEADS)), e)[0:1, :]

    xdt = xs * dt_e
    wx = (xs * wd_e).astype(BF16)

    for g in range(SSM_GROUPS):
        bg = bm[:, g * gs:(g + 1) * gs]
        cg = cm[:, g * gs:(g + 1) * gs]
        cb = lax.dot_general(cg, bg, (((1,), (1,)), ((), ())), preferred_element_type=F32)
        cb = jnp.where(causal, cb, 0.0)
        for hh in range(hg):
            h = g * hg + hh
            seg = a_cs[:, h:h + 1] - a_cs_t[h:h + 1, :]
            mh = (cb * jnp.exp(jnp.minimum(seg, 0.0))).astype(BF16)
            cs = slice(h * SSM_HEAD_DIM, (h + 1) * SSM_HEAD_DIM)
            y_scr[:, cs] = _dot(mh, xdt[:, cs].astype(BF16))
        gsl = slice(g * gw, (g + 1) * gw)
        st = state[:, gsl]
        y_scr[:, gsl] = y_scr[:, gsl] + _dot(cg, st.astype(BF16)) * ea_e[:, gsl]
        upd = lax.dot_general(bg, wx[:, gsl], (((0,), (0,)), ((), ())), preferred_element_type=F32)
        state[:, gsl] = st * dec_e[:, gsl] + upd

    z = z_ref[...]
    y = (y_scr[...] + dvec_ref[...] * xs) * (z * jax.nn.sigmoid(z))
    for g in range(SSM_GROUPS):
        gsl = slice(g * gw, (g + 1) * gw)
        yg = y[:, gsl]
        yg = yg * lax.rsqrt(jnp.mean(yg * yg, axis=-1, keepdims=True) + NORM_EPS)
        y_ref[:, gsl] = (yg * nrm_ref[:, gsl]).astype(y_ref.dtype)

    @pl.when(t == pl.num_programs(1) - 1)
    def _():
        ssm_ref[...] = state[...].T


def _ssd(xbc, misc, z, conv_prev, ssm_prev, wts, *, t_valid):
    b, t, _ = xbc.shape
    assert t % SSD_L == 0
    convp = jnp.pad(conv_prev.astype(F32), ((0, 0), (8 - (CONV_WIDTH - 1), 0), (0, 0)))
    ssmp = ssm_prev.astype(F32).reshape(b, D_INNER, SSM_STATE)
    tok = lambda w: pl.BlockSpec((None, SSD_L, w), lambda bi, ti: (bi, ti, 0))
    full = lambda r, c: pl.BlockSpec((r, c), lambda bi, ti: (0, 0))
    y, ssm = pl.pallas_call(
        functools.partial(_ssd_body, t_valid=t_valid),
        out_shape=[jax.ShapeDtypeStruct((b, t, D_INNER), BF16),
                   jax.ShapeDtypeStruct((b, D_INNER, SSM_STATE), F32)],
        grid=(b, t // SSD_L),
        in_specs=[tok(CONV_DIM),
                  pl.BlockSpec((None, SSD_L, LANES), lambda bi, ti: (bi, ti, MISC_DT // LANES)),
                  tok(D_INNER),
                  pl.BlockSpec((None, 8, CONV_DIM), lambda bi, ti: (bi, 0, 0)),
                  pl.BlockSpec((None, D_INNER, SSM_STATE), lambda bi, ti: (bi, 0, 0)),
                  full(CONV_WIDTH, CONV_DIM), full(1, CONV_DIM), full(1, SSM_HEADS), full(1, SSM_HEADS),
                  full(1, D_INNER), full(1, D_INNER), full(SSM_HEADS, D_INNER)],
        out_specs=[tok(D_INNER),
                   pl.BlockSpec((None, D_INNER, SSM_STATE), lambda bi, ti: (bi, 0, 0))],
        scratch_shapes=[pltpu.VMEM((SSD_L + 8, CONV_DIM), F32),
                        pltpu.VMEM((SSM_STATE, D_INNER), F32),
                        pltpu.VMEM((SSD_L, D_INNER), F32)],
        compiler_params=_cparams(2),
        name="ssd",
    )(xbc, misc, z, convp, ssmp, wts["conv_w"], wts["conv_b"], wts["dt_bias"], wts["a_log"],
      wts["dvec"], wts["ssm_norm"], wts["expand"])
    return y, ssm.reshape(b, SSM_HEADS, SSM_HEAD_DIM, SSM_STATE)


def _mla_prep_body(misc_ref, cos_ref, sin_ref, qn_ref, kvn_ref, w1_ref, w2_ref, q_ref, ckv_ref, kr_ref):
    cos = cos_ref[...]
    sin = sin_ref[...]
    qn = _rms(misc_ref[:, MISC_CQ:MISC_CQ + Q_LORA], qn_ref[...]).astype(BF16)
    q1 = _dot(qn, w1_ref[...])
    q2 = _dot(qn, w2_ref[...])
    scale = (QK_NOPE_DIM + QK_ROPE_DIM) ** -0.5
    for h in range(MLA_HEADS):
        hs = slice(h * LANES, (h + 1) * LANES)
        q_ref[h] = ((q1[:, hs] * cos + q2[:, hs] * sin) * scale).astype(BF16)
    ckv_ref[...] = _rms(misc_ref[:, MISC_CKV:MISC_CKV + KV_LORA], kvn_ref[...])
    kr_ref[...] = misc_ref[:, MISC_KRA:MISC_KRA + LANES] * cos + misc_ref[:, MISC_KRB:MISC_KRB + LANES] * sin


def _mla_prep(misc, cos, sin, n_pos_blocks, wts, *, tm):
    m = misc.shape[0]
    assert m % tm == 0
    return pl.pallas_call(
        _mla_prep_body,
        out_shape=[jax.ShapeDtypeStruct((MLA_HEADS, m, LANES), BF16),
                   jax.ShapeDtypeStruct((m, KV_LORA), F32),
                   jax.ShapeDtypeStruct((m, LANES), F32)],
        grid=(m // tm,),
        in_specs=[pl.BlockSpec((tm, MISC_W), lambda i: (i, 0)),
                  pl.BlockSpec((tm, LANES), lambda i: (i % n_pos_blocks, 0)),
                  pl.BlockSpec((tm, LANES), lambda i: (i % n_pos_blocks, 0)),
                  pl.BlockSpec((1, Q_LORA), lambda i: (0, 0)),
                  pl.BlockSpec((1, KV_LORA), lambda i: (0, 0)),
                  pl.BlockSpec((Q_LORA, MLA_HEADS * LANES), lambda i: (0, 0)),
                  pl.BlockSpec((Q_LORA, MLA_HEADS * LANES), lambda i: (0, 0))],
        out_specs=[pl.BlockSpec((MLA_HEADS, tm, LANES), lambda i: (0, i, 0)),
                   pl.BlockSpec((tm, KV_LORA), lambda i: (i, 0)),
                   pl.BlockSpec((tm, LANES), lambda i: (i, 0))],
        compiler_params=_cparams(1),
        name="mla_prep",
    )(misc, cos, sin, wts["q_norm"], wts["kv_norm"], wts["w_q1"], wts["w_q2"])


def _kv_body(ckv_ref, kr_ref, wk_ref, wv_ref, one_ref, k_ref, v_ref):
    c = ckv_ref[...].astype(BF16)
    k = _dot(c, wk_ref[...])
    v = _dot(c, wv_ref[...])
    kr = kr_ref[...]
    one = one_ref[...]
    for h in range(MLA_HEADS):
        hs = slice(h * LANES, (h + 1) * LANES)
        k_ref[h] = (k[:, hs] + kr).astype(BF16)
        v_ref[h] = (v[:, hs] + one).astype(BF16)


def _kv_expand(ckv, kr128, wts, *, tm):
    m = ckv.shape[0]
    assert m % tm == 0
    hw = MLA_HEADS * LANES
    return pl.pallas_call(
        _kv_body,
        out_shape=[jax.ShapeDtypeStruct((MLA_HEADS, m, LANES), BF16)] * 2,
        grid=(m // tm,),
        in_specs=[pl.BlockSpec((tm, KV_LORA), lambda i: (i, 0)),
                  pl.BlockSpec((tm, LANES), lambda i: (i, 0)),
                  pl.BlockSpec((KV_LORA, hw), lambda i: (0, 0)),
                  pl.BlockSpec((KV_LORA, hw), lambda i: (0, 0)),
                  pl.BlockSpec((1, LANES), lambda i: (0, 0))],
        out_specs=[pl.BlockSpec((MLA_HEADS, tm, LANES), lambda i: (0, i, 0))] * 2,
        compiler_params=_cparams(1),
        name="kv_expand",
    )(ckv, kr128, wts["w_k"], wts["w_v"], wts["one_col"])


def _attn_body(q_ref, k_ref, v_ref, kp_ref, vp_ref, o_ref, acc_scr, m_scr, *, causal, prefix_valid, kv_valid):
    i = pl.program_id(1)
    j = pl.program_id(2)
    tq = q_ref.shape[1]
    tk = k_ref.shape[1]
    tp = kp_ref.shape[1]

    def step(kr, vr, mask):
        w = kr.shape[1]

        def body(h, carry):
            s = lax.dot_general(q_ref[h], kr[h], (((1,), (1,)), ((), ())), preferred_element_type=F32)
            if mask is not None:
                s = jnp.where(mask, s, NEG_BIG)
            m_prev = m_scr[h]
            m_new = jnp.maximum(m_prev, jnp.max(s, axis=1, keepdims=True))
            alpha = jnp.exp(m_prev - m_new)
            p = jnp.exp(s - pltpu.repeat(m_new, w // LANES, 1)).astype(BF16)
            acc_scr[h] = acc_scr[h] * alpha + _dot(p, vr[h])
            m_scr[h] = m_new
            return carry

        lax.fori_loop(0, MLA_HEADS, body, 0)

    def finalize():
        def body(h, carry):
            acc = acc_scr[h]
            o_ref[h] = (acc / acc[:, V_HEAD_DIM:V_HEAD_DIM + 1]).astype(o_ref.dtype)
            return carry

        lax.fori_loop(0, MLA_HEADS, body, 0)

    @pl.when(j == 0)
    def _():
        acc_scr[...] = jnp.zeros_like(acc_scr)
        m_scr[...] = jnp.full_like(m_scr, NEG_BIG)
        step(kp_ref, vp_ref, lax.broadcasted_iota(jnp.int32, (tq, tp), 1) < prefix_valid)

    if causal:
        @pl.when(j < i)
        def _():
            step(k_ref, v_ref, None)

        @pl.when(j == i)
        def _():
            qc = lax.broadcasted_iota(jnp.int32, (tq, tk), 0) // CHUNK
            kc = lax.broadcasted_iota(jnp.int32, (tq, tk), 1) // CHUNK
            step(k_ref, v_ref, kc <= qc)
            finalize()
    else:
        mask = None
        if kv_valid is not None:
            mask = (j * tk + lax.broadcasted_iota(jnp.int32, (tq, tk), 1)) < kv_valid
        step(k_ref, v_ref, mask)

        @pl.when(j == pl.num_programs(2) - 1)
        def _():
            finalize()


def _attention(q, k, v, kp, vp, *, causal, tq, tk, prefix_valid, kv_valid=None):
    hh, b, t, _ = q.shape
    tkv = k.shape[2]
    assert t % tq == 0 and tkv % tk == 0
    if causal:
        assert tq == tk and t == tkv and tq % CHUNK == 0
        kv_idx = lambda bi, i, j: (0, bi, jnp.minimum(i, j), 0)
    else:
        kv_idx = lambda bi, i, j: (0, bi, j, 0)
    tp = kp.shape[1]
    return pl.pallas_call(
        functools.partial(_attn_body, causal=causal, prefix_valid=prefix_valid, kv_valid=kv_valid),
        out_shape=jax.ShapeDtypeStruct((hh, b, t, LANES), BF16),
        grid=(b, t // tq, tkv // tk),
        in_specs=[pl.BlockSpec((hh, None, tq, LANES), lambda bi, i, j: (0, bi, i, 0)),
                  pl.BlockSpec((hh, None, tk, LANES), kv_idx),
                  pl.BlockSpec((hh, None, tk, LANES), kv_idx),
                  pl.BlockSpec((hh, tp, LANES), lambda bi, i, j: (0, 0, 0)),
                  pl.BlockSpec((hh, tp, LANES), lambda bi, i, j: (0, 0, 0))],
        out_specs=pl.BlockSpec((hh, None, tq, LANES), lambda bi, i, j: (0, bi, i, 0)),
        scratch_shapes=[pltpu.VMEM((hh, tq, LANES), F32), pltpu.VMEM((hh, tq, LANES), F32)],
        compiler_params=_cparams(3),
        name="attention",
    )(q, k, v, kp, vp)


def _mix_body(h_ref, y_ref, o_ref, gate_ref, gb_ref, ws_ref, wm_ref, wx_ref, out_ref):
    y_s = _dot(y_ref[...], ws_ref[...])
    y_m = _dot(o_ref[0], wm_ref[0])
    for h in range(1, MLA_HEADS):
        y_m = y_m + _dot(o_ref[h], wm_ref[h])
    gates = jax.nn.sigmoid(gate_ref[...] + gb_ref[...])
    merged = gates[:, :D_MODEL] * y_s + gates[:, D_MODEL:] * y_m
    out_ref[...] = h_ref[...] + _dot(merged.astype(BF16), wx_ref[...])


def _mix(h, y, o, gate, wts, *, tm_pref=256):
    m = h.shape[0]
    tm = _row_tile(m, tm_pref)
    return pl.pallas_call(
        _mix_body,
        out_shape=jax.ShapeDtypeStruct((m, D_MODEL), F32),
        grid=(m // tm,),
        in_specs=[pl.BlockSpec((tm, D_MODEL), lambda i: (i, 0)),
                  pl.BlockSpec((tm, D_INNER), lambda i: (i, 0)),
                  pl.BlockSpec((MLA_HEADS, tm, LANES), lambda i: (0, i, 0)),
                  pl.BlockSpec((tm, 2 * D_MODEL), lambda i: (i, 0)),
                  pl.BlockSpec((1, 2 * D_MODEL), lambda i: (0, 0)),
                  pl.BlockSpec((D_INNER, D_MODEL), lambda i: (0, 0)),
                  pl.BlockSpec((MLA_HEADS, LANES, D_MODEL), lambda i: (0, 0, 0)),
                  pl.BlockSpec((D_MODEL, D_MODEL), lambda i: (0, 0))],
        out_specs=pl.BlockSpec((tm, D_MODEL), lambda i: (i, 0)),
        compiler_params=_cparams(1),
        name="mix",
    )(h, y, o, gate, wts["gate_bias"], wts["w_ssm_out"], wts["w_mla_out"], wts["w_mix_out"])


def _prep_weights(p):
    w_in = p["w_in"]
    w_kr = w_in[:, OFF_KROPE:OFF_GATE]
    half = QK_ROPE_DIM // 2
    w_kr_sw = jnp.concatenate([w_kr[:, half:], w_kr[:, :half]], axis=1)
    zc = lambda n: jnp.zeros((D_MODEL, n), F32)
    w_misc = jnp.concatenate([
        w_in[:, OFF_CQ:OFF_CKV], w_in[:, OFF_CKV:OFF_KROPE],
        zc(QK_NOPE_DIM), w_kr, zc(LANES - QK_NOPE_DIM - QK_ROPE_DIM),
        zc(QK_NOPE_DIM), w_kr_sw, zc(LANES - QK_NOPE_DIM - QK_ROPE_DIM),
        w_in[:, OFF_DT:OFF_CQ], zc(LANES - SSM_HEADS)], axis=1)

    wq = p["w_uq"].reshape(Q_LORA, MLA_HEADS, QK_NOPE_DIM + QK_ROPE_DIM)
    q_nope, q_rope = wq[..., :QK_NOPE_DIM], wq[..., QK_NOPE_DIM:]
    q_rope_sw = jnp.concatenate([q_rope[..., half:], q_rope[..., :half]], axis=-1)
    zq = lambda n: jnp.zeros((Q_LORA, MLA_HEADS, n), F32)
    pad = LANES - QK_NOPE_DIM - QK_ROPE_DIM
    w_q1 = jnp.concatenate([q_nope, q_rope, zq(pad)], axis=-1).reshape(Q_LORA, MLA_HEADS * LANES)
    w_q2 = jnp.concatenate([zq(QK_NOPE_DIM), q_rope_sw, zq(pad)], axis=-1).reshape(Q_LORA, MLA_HEADS * LANES)

    wkv = p["w_ukv"].reshape(KV_LORA, MLA_HEADS, QK_NOPE_DIM + V_HEAD_DIM)
    zk = jnp.zeros((KV_LORA, MLA_HEADS, LANES - QK_NOPE_DIM), F32)
    w_k = jnp.concatenate([wkv[..., :QK_NOPE_DIM], zk], axis=-1).reshape(KV_LORA, MLA_HEADS * LANES)
    zv = jnp.zeros((KV_LORA, MLA_HEADS, LANES - V_HEAD_DIM), F32)
    w_v = jnp.concatenate([wkv[..., QK_NOPE_DIM:], zv], axis=-1).reshape(KV_LORA, MLA_HEADS * LANES)

    wmo = p["mla_w_out"].reshape(MLA_HEADS, V_HEAD_DIM, D_MODEL)
    w_mla_out = jnp.concatenate([wmo, jnp.zeros((MLA_HEADS, LANES - V_HEAD_DIM, D_MODEL), F32)], axis=1)

    expand = jnp.repeat(jnp.eye(SSM_HEADS, dtype=F32), SSM_HEAD_DIM, axis=1)
    one_col = (jnp.arange(LANES) == V_HEAD_DIM).astype(F32).reshape(1, LANES)
    return {
        "w_z": w_in[:, OFF_Z:OFF_XBC].astype(BF16),
        "w_xbc": w_in[:, OFF_XBC:OFF_DT].astype(BF16),
        "w_gate": w_in[:, OFF_GATE:].astype(BF16),
        "w_misc": w_misc.astype(BF16),
        "w_q1": w_q1.astype(BF16), "w_q2": w_q2.astype(BF16),
        "w_k": w_k.astype(BF16), "w_v": w_v.astype(BF16),
        "w_mla_out": w_mla_out.astype(BF16),
        "w_ssm_out": p["ssm_w_out"].astype(BF16),
        "w_mix_out": p["w_mix_out"].astype(BF16),
        "ffn1_w_gu": p["ffn1_w_gu"].astype(BF16), "ffn1_w_down": p["ffn1_w_down"].astype(BF16),
        "ffn2_w_gu": p["ffn2_w_gu"].astype(BF16), "ffn2_w_down": p["ffn2_w_down"].astype(BF16),
        "gate_bias": p["gate_bias"].reshape(1, -1),
        "conv_w": p["conv_w"], "conv_b": p["conv_b"].reshape(1, -1),
        "dt_bias": p["dt_bias"].reshape(1, -1), "a_log": p["a_log"].reshape(1, -1),
        "dvec": jnp.repeat(p["ssm_d"], SSM_HEAD_DIM).reshape(1, -1),
        "ssm_norm": p["ssm_norm"].reshape(1, -1),
        "expand": expand.astype(BF16), "one_col": one_col,
        "q_norm": p["q_norm"].reshape(1, -1), "kv_norm": p["kv_norm"].reshape(1, -1),
    }


def _rope_tables(pos):
    half = QK_ROPE_DIM // 2
    inv = ROPE_THETA ** (-jnp.arange(half, dtype=F32) / half)
    ang = pos.astype(F32)[:, None] * inv[None, :]
    c, s = jnp.cos(ang), jnp.sin(ang)
    t = pos.shape[0]
    pad = jnp.zeros((t, LANES - QK_NOPE_DIM - QK_ROPE_DIM), F32)
    cos = jnp.concatenate([jnp.ones((t, QK_NOPE_DIM), F32), c, c, pad], axis=1)
    sin = jnp.concatenate([jnp.zeros((t, QK_NOPE_DIM), F32), -s, s, pad], axis=1)
    return cos, sin


def _pad_rows(a, axis, to):
    padw = [(0, 0)] * a.ndim
    padw[axis] = (0, to - a.shape[axis])
    return jnp.pad(a, padw)


def _in_proj(u, wts):
    return (_matmul(u, wts["w_z"], F32), _matmul(u, wts["w_xbc"], F32),
            _matmul(u, wts["w_gate"], F32), _matmul(u, wts["w_misc"], F32))


def kernel(x_prompt, x_sample, cache_ckv, cache_krope, state_ssm, state_conv, meta_tokens, ffn1_norm, ffn1_w_gu, ffn1_w_down, mix_norm, w_in, gate_bias, conv_w, conv_b, dt_bias, a_log, ssm_d, ssm_norm, ssm_w_out, q_norm, w_uq, kv_norm, w_ukv, mla_w_out, w_mix_out, ffn2_norm, ffn2_w_gu, ffn2_w_down, final_norm):
    b, seq, _ = x_prompt.shape
    db, dseq, _ = x_sample.shape
    past = cache_ckv.shape[2]
    n_meta = meta_tokens.shape[0]
    nd = db * dseq
    wts = _prep_weights({
        "w_in": w_in[0], "w_uq": w_uq[0], "w_ukv": w_ukv[0], "mla_w_out": mla_w_out[0],
        "ssm_w_out": ssm_w_out[0], "w_mix_out": w_mix_out[0],
        "ffn1_w_gu": ffn1_w_gu[0], "ffn1_w_down": ffn1_w_down[0],
        "ffn2_w_gu": ffn2_w_gu[0], "ffn2_w_down": ffn2_w_down[0],
        "gate_bias": gate_bias[0], "conv_w": conv_w[0], "conv_b": conv_b[0], "dt_bias": dt_bias[0],
        "a_log": a_log[0], "ssm_d": ssm_d[0], "ssm_norm": ssm_norm[0],
        "q_norm": q_norm[0], "kv_norm": kv_norm[0]})

    xb = x_prompt.reshape(b * seq, D_MODEL)
    xs = jnp.concatenate([x_sample.reshape(nd, D_MODEL), meta_tokens.astype(F32)], axis=0)

    hb, ub = _ffn(xb, ffn1_norm[0], wts["ffn1_w_gu"], wts["ffn1_w_down"], mix_norm[0], emit_h=True, out_dtype=BF16)
    hs, us = _ffn(xs, ffn1_norm[0], wts["ffn1_w_gu"], wts["ffn1_w_down"], mix_norm[0], emit_h=True, out_dtype=BF16)
    zb, xbcb, gateb, miscb = _in_proj(ub, wts)
    zs, xbcs, gates, miscs = _in_proj(us, wts)

    L = SSD_L
    meta3 = lambda a: _pad_rows(a[nd:][None], 1, L)
    _, ssm_meta = _ssd(meta3(xbcs), meta3(miscs), meta3(zs),
                       jnp.zeros((1, CONV_WIDTH - 1, CONV_DIM), F32),
                       jnp.zeros((1, SSM_HEADS, SSM_HEAD_DIM, SSM_STATE), F32), wts, t_valid=n_meta)
    conv_meta = xbcs[nd + n_meta - (CONV_WIDTH - 1):nd + n_meta][None]
    yb, ssm_p = _ssd(xbcb.reshape(b, seq, -1), miscb.reshape(b, seq, -1), zb.reshape(b, seq, -1),
                     jnp.broadcast_to(conv_meta, (b,) + conv_meta.shape[1:]),
                     jnp.broadcast_to(ssm_meta, (b,) + ssm_meta.shape[1:]), wts, t_valid=None)
    dec3 = lambda a: _pad_rows(a[:nd].reshape(db, dseq, -1), 1, L)
    yd, ssm_s = _ssd(dec3(xbcs), dec3(miscs), dec3(zs), state_conv[0], state_ssm[0], wts, t_valid=dseq)
    yd = yd[:, :dseq].reshape(nd, D_INNER)
    conv_p = xbcb.reshape(b, seq, -1)[:, seq - (CONV_WIDTH - 1):]
    xbcd = xbcs[:nd].reshape(db, dseq, -1)
    conv_s = jnp.concatenate([state_conv[0].astype(F32), xbcd], axis=1)[:, -(CONV_WIDTH - 1):]

    tq = _row_tile(seq, 512)
    cos_b, sin_b = _rope_tables(jnp.arange(seq, dtype=jnp.int32))
    pos_s = jnp.concatenate([jnp.tile(past + jnp.arange(dseq, dtype=jnp.int32), db),
                             jnp.arange(n_meta, dtype=jnp.int32) - n_meta])
    cos_s, sin_s = _rope_tables(pos_s)
    qb, ckvb, krb = _mla_prep(miscb, cos_b, sin_b, seq // tq, wts, tm=tq)
    qs, ckvs, krs = _mla_prep(miscs, cos_s, sin_s, 1, wts, tm=miscs.shape[0])
    kb, vb = _kv_expand(ckvb, krb, wts, tm=tq)
    kp, vp = _kv_expand(_pad_rows(ckvs[nd:], 0, LANES), _pad_rows(krs[nd:], 0, LANES), wts, tm=LANES)
    tkd = -(-(past + dseq) // LANES) * LANES
    ckv_d = _pad_rows(jnp.concatenate([cache_ckv[0].astype(F32), ckvs[:nd].reshape(db, dseq, -1)], axis=1), 1, tkd)
    kr_cache = jnp.pad(cache_krope[0].astype(F32),
                       ((0, 0), (0, 0), (QK_NOPE_DIM, LANES - QK_NOPE_DIM - QK_ROPE_DIM)))
    kr_d = _pad_rows(jnp.concatenate([kr_cache, krs[:nd].reshape(db, dseq, -1)], axis=1), 1, tkd)
    kd, vd = _kv_expand(ckv_d.reshape(db * tkd, -1), kr_d.reshape(db * tkd, -1), wts,
                        tm=_row_tile(db * tkd, 512))

    hsplit = lambda a, nb: a.reshape(MLA_HEADS, nb, -1, LANES)
    ob = _attention(hsplit(qb, b), hsplit(kb, b), hsplit(vb, b), kp, vp,
                    causal=True, tq=tq, tk=tq, prefix_valid=n_meta)
    od = _attention(hsplit(qs[:, :nd], db), hsplit(kd, db), hsplit(vd, db), kp, vp,
                    causal=False, tq=dseq, tk=tkd, prefix_valid=n_meta, kv_valid=past + dseq)

    h2b = _mix(hb, yb.reshape(b * seq, -1), ob.reshape(MLA_HEADS, b * seq, LANES), gateb, wts)
    h2d = _mix(hs[:nd], yd, od.reshape(MLA_HEADS, nd, LANES), gates[:nd], wts)
    (y_p,) = _ffn(h2b, ffn2_norm[0], wts["ffn2_w_gu"], wts["ffn2_w_down"], final_norm, emit_h=False, out_dtype=F32)
    (y_s,) = _ffn(h2d, ffn2_norm[0], wts["ffn2_w_gu"], wts["ffn2_w_down"], final_norm, emit_h=False, out_dtype=F32)

    kr32 = lambda a: a[..., QK_NOPE_DIM:QK_NOPE_DIM + QK_ROPE_DIM]
    ckv_meta = jnp.broadcast_to(ckvs[nd:][None], (b, n_meta, KV_LORA))
    kr_meta = jnp.broadcast_to(kr32(krs[nd:])[None], (b, n_meta, QK_ROPE_DIM))
    ckv_p = jnp.concatenate([ckv_meta, ckvb.reshape(b, seq, -1)], axis=1)
    kr_p = jnp.concatenate([kr_meta, kr32(krb).reshape(b, seq, -1)], axis=1)
    return (y_p.reshape(b, seq, D_MODEL), y_s.reshape(db, dseq, D_MODEL),
            ckv_p[None], kr_p[None], ssm_p[None], conv_p[None],
            ckvs[:nd].reshape(db, dseq, -1)[None], kr32(krs[:nd]).reshape(db, dseq, -1)[None],
            ssm_s[None], conv_s[None])
```

```python
import functools
import math

import jax
import jax.numpy as jnp
from jax import lax
from jax.experimental import pallas as pl
from jax.experimental.pallas import tpu as pltpu

F32 = jnp.float32
BF16 = jnp.bfloat16

D_MODEL = 1024
CHUNK = 64
FFN_DIM = 2816
NORM_EPS = 1e-6
D_INNER = 2048
SSM_HEAD_DIM = 64
SSM_HEADS = 32
SSM_GROUPS = 4
SSM_STATE = 128
CONV_WIDTH = 4
CONV_DIM = D_INNER + 2 * SSM_GROUPS * SSM_STATE
MLA_HEADS = 16
Q_LORA = 384
KV_LORA = 256
QK_NOPE_DIM = 64
QK_ROPE_DIM = 32
V_HEAD_DIM = 64
ROPE_THETA = 10000.0
OFF_Z = 0
OFF_XBC = OFF_Z + D_INNER
OFF_DT = OFF_XBC + CONV_DIM
OFF_CQ = OFF_DT + SSM_HEADS
OFF_CKV = OFF_CQ + Q_LORA
OFF_KROPE = OFF_CKV + KV_LORA
OFF_GATE = OFF_KROPE + QK_ROPE_DIM

LANES = 128
SSD_L = 128
FFN_TF = 256
HEAD_UNROLL = 4
ATTN_TQ = 1024
NEG_BIG = -1e30
VMEM_LIMIT = 52 * 1024 * 1024

MISC_CQ = 0
MISC_CKV = 384
MISC_KRA = 640
MISC_KRB = 768
MISC_DT = 896
MISC_W = 1024


def _cparams(n_grid):
    return pltpu.CompilerParams(dimension_semantics=("arbitrary",) * n_grid,
                                vmem_limit_bytes=VMEM_LIMIT)


def _row_tile(m, pref):
    if m <= pref:
        return m
    t = pref
    while t >= 16:
        if m % t == 0:
            return t
        t //= 2
    return m


def _rms(x, g):
    return x * lax.rsqrt(jnp.mean(x * x, axis=-1, keepdims=True) + NORM_EPS) * g


def _dot(a, b):
    return jnp.dot(a, b, preferred_element_type=F32)


def _ffn_body(x_ref, g1_ref, wg_ref, wu_ref, wd_ref, g2_ref, *rest, emit_h):
    if emit_h:
        h_ref, n_ref, xn_scr, acc_scr = rest
    else:
        n_ref, xn_scr, acc_scr = rest
    f = pl.program_id(1)

    @pl.when(f == 0)
    def _():
        xn_scr[...] = _rms(x_ref[...], g1_ref[...]).astype(BF16)
        acc_scr[...] = jnp.zeros_like(acc_scr)

    xn = xn_scr[...]
    g = _dot(xn, wg_ref[...])
    u = _dot(xn, wu_ref[...])
    act = (g * jax.nn.sigmoid(g) * u).astype(BF16)
    acc_scr[...] += _dot(act, wd_ref[...])

    @pl.when(f == pl.num_programs(1) - 1)
    def _():
        h = x_ref[...] + 0.5 * acc_scr[...]
        if emit_h:
            h_ref[...] = h
        n_ref[...] = _rms(h, g2_ref[...]).astype(n_ref.dtype)


def _ffn(x, g1, w_gu, w_down, g2, *, emit_h, out_dtype, tm_pref=1024):
    m = x.shape[0]
    tm = _row_tile(m, tm_pref)
    nf = FFN_DIM // FFN_TF
    row = pl.BlockSpec((tm, D_MODEL), lambda i, f: (i, 0))
    vec = pl.BlockSpec((1, D_MODEL), lambda i, f: (0, 0))
    out_shape = [jax.ShapeDtypeStruct((m, D_MODEL), out_dtype)]
    out_specs = [row]
    if emit_h:
        out_shape = [jax.ShapeDtypeStruct((m, D_MODEL), F32)] + out_shape
        out_specs = [row, row]
    return pl.pallas_call(
        functools.partial(_ffn_body, emit_h=emit_h),
        out_shape=out_shape,
        grid=(m // tm, nf),
        in_specs=[row, vec,
                  pl.BlockSpec((D_MODEL, FFN_TF), lambda i, f: (0, f)),
                  pl.BlockSpec((D_MODEL, FFN_TF), lambda i, f: (0, f + nf)),
                  pl.BlockSpec((FFN_TF, D_MODEL), lambda i, f: (f, 0)),
                  vec],
        out_specs=out_specs,
        scratch_shapes=[pltpu.VMEM((tm, D_MODEL), BF16), pltpu.VMEM((tm, D_MODEL), F32)],
        compiler_params=_cparams(2),
        name="ffn",
    )(x, g1.reshape(1, -1), w_gu, w_gu, w_down, g2.reshape(1, -1))


def _mm_body(x_ref, w_ref, b_ref, o_ref, *, act):
    y = _dot(x_ref[...], w_ref[...]) + b_ref[...]
    if act == "silu":
        y = y * jax.nn.sigmoid(y)
    elif act == "sigmoid":
        y = jax.nn.sigmoid(y)
    o_ref[...] = y.astype(o_ref.dtype)


def _matmul(x, w, out_dtype, *, bias=None, act=None, tm_pref=1024, tn=512):
    m, k = x.shape
    n = w.shape[1]
    tm = _row_tile(m, tm_pref)
    if bias is None:
        bias = jnp.zeros((1, n), F32)
    return pl.pallas_call(
        functools.partial(_mm_body, act=act),
        out_shape=jax.ShapeDtypeStruct((m, n), out_dtype),
        grid=(m // tm, n // tn),
        in_specs=[pl.BlockSpec((tm, k), lambda i, j: (i, 0)),
                  pl.BlockSpec((k, tn), lambda i, j: (0, j)),
                  pl.BlockSpec((1, tn), lambda i, j: (0, j))],
        out_specs=pl.BlockSpec((tm, tn), lambda i, j: (i, j)),
        compiler_params=_cparams(2),
        name="proj",
    )(x, w, bias)


def _softplus(x):
    return jnp.maximum(x, 0.0) + jnp.log1p(jnp.exp(-jnp.abs(x)))


def _expand_heads(x, e):
    hi = x.astype(BF16)
    lo = (x - hi.astype(F32)).astype(BF16)
    return _dot(hi, e) + _dot(lo, e)


def _ssd_body(xbc_ref, dtb_ref, zs_ref, convp_ref, ssmp_ref, cw_ref, cb_ref, dtbias_ref, alog_ref,
              dvec_ref, nrm_ref, e_ref, y_ref, ssm_ref, xpad, state, y_scr, *, t_valid):
    t = pl.program_id(1)
    L = SSD_L
    gs = SSM_STATE
    hg = SSM_HEADS // SSM_GROUPS
    gw = hg * SSM_HEAD_DIM

    @pl.when(t == 0)
    def _():
        xpad[0:8, :] = convp_ref[...]
        state[...] = ssmp_ref[...].T

    xpad[8:8 + L, :] = xbc_ref[...]
    conv = cb_ref[...] + xpad[5:5 + L, :] * cw_ref[0:1, :]
    conv = conv + xpad[6:6 + L, :] * cw_ref[1:2, :]
    conv = conv + xpad[7:7 + L, :] * cw_ref[2:3, :]
    conv = conv + xpad[8:8 + L, :] * cw_ref[3:4, :]
    xpad[0:8, :] = xpad[L:L + 8, :]
    xc = conv * jax.nn.sigmoid(conv)
    xs = xc[:, :D_INNER]
    bm = xc[:, D_INNER:D_INNER + SSM_GROUPS * gs].astype(BF16)
    cm = xc[:, D_INNER + SSM_GROUPS * gs:].astype(BF16)

    dt = _softplus(dtb_ref[:, 0:SSM_HEADS] + dtbias_ref[...])
    if t_valid is not None:
        rows = t * L + lax.broadcasted_iota(jnp.int32, (L, SSM_HEADS), 0)
        dt = jnp.where(rows < t_valid, dt, 0.0)
    da = dt * (-jnp.exp(alog_ref[...]))
    ri = lax.broadcasted_iota(jnp.int32, (L, L), 0)
    ci = lax.broadcasted_iota(jnp.int32, (L, L), 1)
    causal = ri >= ci
    a_cs = jnp.dot(causal.astype(F32), da, precision=lax.Precision.HIGHEST,
                   preferred_element_type=F32)
    eye = (lax.broadcasted_iota(jnp.int32, (SSM_HEADS, SSM_HEADS), 0)
           == lax.broadcasted_iota(jnp.int32, (SSM_HEADS, SSM_HEADS), 1)).astype(F32)
    a_cs_t = lax.dot_general(eye, a_cs, (((1,), (1,)), ((), ())), precision=lax.Precision.HIGHEST,
                             preferred_element_type=F32)
    total = a_cs[L - 1:L, :]

    e = e_ref[...]
    dt_e = _expand_heads(dt, e)
    ea_e = _expand_heads(jnp.exp(a_cs), e)
    wd_e = _expand_heads(jnp.exp(total - a_cs) * dt, e)
    dec_e = _expand_heads(jnp.broadcast_to(jnp.exp(total), (8, SSM_HEADS)), e)[0:1, :]

    xdt = xs * dt_e
    wx = (xs * wd_e).astype(BF16)

    for g in range(SSM_GROUPS):
        bg = bm[:, g * gs:(g + 1) * gs]
        cg = cm[:, g * gs:(g + 1) * gs]
        cb = lax.dot_general(cg, bg, (((1,), (1,)), ((), ())), preferred_element_type=F32)
        cb = jnp.where(causal, cb, 0.0)
        for hh in range(hg):
            h = g * hg + hh
            seg = a_cs[:, h:h + 1] - a_cs_t[h:h + 1, :]
            mh = (cb * jnp.exp(jnp.minimum(seg, 0.0))).astype(BF16)
            cs = slice(h * SSM_HEAD_DIM, (h + 1) * SSM_HEAD_DIM)
            y_scr[:, cs] = _dot(mh, xdt[:, cs].astype(BF16))
        gsl = slice(g * gw, (g + 1) * gw)
        st = state[:, gsl]
        y_scr[:, gsl] = y_scr[:, gsl] + _dot(cg, st.astype(BF16)) * ea_e[:, gsl]
        upd = lax.dot_general(bg, wx[:, gsl], (((0,), (0,)), ((), ())), preferred_element_type=F32)
        state[:, gsl] = st * dec_e[:, gsl] + upd

    y = (y_scr[...] + dvec_ref[...] * xs) * zs_ref[...].astype(F32)
    for g in range(SSM_GROUPS):
        gsl = slice(g * gw, (g + 1) * gw)
        yg = y[:, gsl]
        yg = yg * lax.rsqrt(jnp.mean(yg * yg, axis=-1, keepdims=True) + NORM_EPS)
        y_ref[:, gsl] = (yg * nrm_ref[:, gsl]).astype(y_ref.dtype)

    @pl.when(t == pl.num_programs(1) - 1)
    def _():
        ssm_ref[...] = state[...].T


def _ssd(xbc, misc, z, conv_prev, ssm_prev, wts, *, t_valid):
    b, t, _ = xbc.shape
    assert t % SSD_L == 0
    convp = jnp.pad(conv_prev.astype(F32), ((0, 0), (8 - (CONV_WIDTH - 1), 0), (0, 0)))
    ssmp = ssm_prev.astype(F32).reshape(b, D_INNER, SSM_STATE)
    tok = lambda w: pl.BlockSpec((None, SSD_L, w), lambda bi, ti: (bi, ti, 0))
    full = lambda r, c: pl.BlockSpec((r, c), lambda bi, ti: (0, 0))
    y, ssm = pl.pallas_call(
        functools.partial(_ssd_body, t_valid=t_valid),
        out_shape=[jax.ShapeDtypeStruct((b, t, D_INNER), BF16),
                   jax.ShapeDtypeStruct((b, D_INNER, SSM_STATE), F32)],
        grid=(b, t // SSD_L),
        in_specs=[tok(CONV_DIM),
                  pl.BlockSpec((None, SSD_L, LANES), lambda bi, ti: (bi, ti, MISC_DT // LANES)),
                  tok(D_INNER),
                  pl.BlockSpec((None, 8, CONV_DIM), lambda bi, ti: (bi, 0, 0)),
                  pl.BlockSpec((None, D_INNER, SSM_STATE), lambda bi, ti: (bi, 0, 0)),
                  full(CONV_WIDTH, CONV_DIM), full(1, CONV_DIM), full(1, SSM_HEADS), full(1, SSM_HEADS),
                  full(1, D_INNER), full(1, D_INNER), full(SSM_HEADS, D_INNER)],
        out_specs=[tok(D_INNER),
                   pl.BlockSpec((None, D_INNER, SSM_STATE), lambda bi, ti: (bi, 0, 0))],
        scratch_shapes=[pltpu.VMEM((SSD_L + 8, CONV_DIM), F32),
                        pltpu.VMEM((SSM_STATE, D_INNER), F32),
                        pltpu.VMEM((SSD_L, D_INNER), F32)],
        compiler_params=_cparams(2),
        name="ssd",
    )(xbc, misc, z, convp, ssmp, wts["conv_w"], wts["conv_b"], wts["dt_bias"], wts["a_log"],
      wts["dvec"], wts["ssm_norm"], wts["expand"])
    return y, ssm.reshape(b, SSM_HEADS, SSM_HEAD_DIM, SSM_STATE)


def _mla_prep_body(misc_ref, cos_ref, sin_ref, qn_ref, kvn_ref, w1_ref, w2_ref, q_ref, ckv_ref, kr_ref):
    cos = cos_ref[...]
    sin = sin_ref[...]
    qn = _rms(misc_ref[:, MISC_CQ:MISC_CQ + Q_LORA], qn_ref[...]).astype(BF16)
    q1 = _dot(qn, w1_ref[...])
    q2 = _dot(qn, w2_ref[...])
    scale = (QK_NOPE_DIM + QK_ROPE_DIM) ** -0.5
    for h in range(MLA_HEADS):
        hs = slice(h * LANES, (h + 1) * LANES)
        q_ref[h] = ((q1[:, hs] * cos + q2[:, hs] * sin) * scale).astype(BF16)
    ckv_ref[...] = _rms(misc_ref[:, MISC_CKV:MISC_CKV + KV_LORA], kvn_ref[...])
    kr_ref[...] = misc_ref[:, MISC_KRA:MISC_KRA + LANES] * cos + misc_ref[:, MISC_KRB:MISC_KRB + LANES] * sin


def _mla_prep(misc, cos, sin, n_pos_blocks, wts, *, tm):
    m = misc.shape[0]
    assert m % tm == 0
    return pl.pallas_call(
        _mla_prep_body,
        out_shape=[jax.ShapeDtypeStruct((MLA_HEADS, m, LANES), BF16),
                   jax.ShapeDtypeStruct((m, KV_LORA), F32),
                   jax.ShapeDtypeStruct((m, LANES), F32)],
        grid=(m // tm,),
        in_specs=[pl.BlockSpec((tm, MISC_W), lambda i: (i, 0)),
                  pl.BlockSpec((tm, LANES), lambda i: (i % n_pos_blocks, 0)),
                  pl.BlockSpec((tm, LANES), lambda i: (i % n_pos_blocks, 0)),
                  pl.BlockSpec((1, Q_LORA), lambda i: (0, 0)),
                  pl.BlockSpec((1, KV_LORA), lambda i: (0, 0)),
                  pl.BlockSpec((Q_LORA, MLA_HEADS * LANES), lambda i: (0, 0)),
                  pl.BlockSpec((Q_LORA, MLA_HEADS * LANES), lambda i: (0, 0))],
        out_specs=[pl.BlockSpec((MLA_HEADS, tm, LANES), lambda i: (0, i, 0)),
                   pl.BlockSpec((tm, KV_LORA), lambda i: (i, 0)),
                   pl.BlockSpec((tm, LANES), lambda i: (i, 0))],
        compiler_params=_cparams(1),
        name="mla_prep",
    )(misc, cos, sin, wts["q_norm"], wts["kv_norm"], wts["w_q1"], wts["w_q2"])


def _kv_body(ckv_ref, kr_ref, wk_ref, wv_ref, one_ref, k_ref, v_ref):
    c = ckv_ref[...].astype(BF16)
    k = _dot(c, wk_ref[...])
    v = _dot(c, wv_ref[...])
    kr = kr_ref[...]
    one = one_ref[...]
    for h in range(MLA_HEADS):
        hs = slice(h * LANES, (h + 1) * LANES)
        k_ref[h] = (k[:, hs] + kr).astype(BF16)
        v_ref[h] = (v[:, hs] + one).astype(BF16)


def _kv_expand(ckv, kr128, wts, *, tm):
    m = ckv.shape[0]
    assert m % tm == 0
    hw = MLA_HEADS * LANES
    return pl.pallas_call(
        _kv_body,
        out_shape=[jax.ShapeDtypeStruct((MLA_HEADS, m, LANES), BF16)] * 2,
        grid=(m // tm,),
        in_specs=[pl.BlockSpec((tm, KV_LORA), lambda i: (i, 0)),
                  pl.BlockSpec((tm, LANES), lambda i: (i, 0)),
                  pl.BlockSpec((KV_LORA, hw), lambda i: (0, 0)),
                  pl.BlockSpec((KV_LORA, hw), lambda i: (0, 0)),
                  pl.BlockSpec((1, LANES), lambda i: (0, 0))],
        out_specs=[pl.BlockSpec((MLA_HEADS, tm, LANES), lambda i: (0, i, 0))] * 2,
        compiler_params=_cparams(1),
        name="kv_expand",
    )(ckv, kr128, wts["w_k"], wts["w_v"], wts["one_col"])


def _attn_body(q_ref, k_ref, v_ref, kp_ref, vp_ref, o_ref, *scr, causal, prefix_valid, kv_valid):
    acc_scrs, m_scrs = scr[:HEAD_UNROLL], scr[HEAD_UNROLL:]
    i = pl.program_id(1)
    j = pl.program_id(2)
    tq = q_ref.shape[1]
    tk = k_ref.shape[1]
    tp = kp_ref.shape[1]
    n_trips = MLA_HEADS // HEAD_UNROLL

    def step(kr, vr, mask):
        w = kr.shape[1]

        def body(hp, carry):
            for u in range(HEAD_UNROLL):
                h = hp * HEAD_UNROLL + u
                acc_scr, m_scr = acc_scrs[u], m_scrs[u]
                s = lax.dot_general(q_ref[h], kr[h], (((1,), (1,)), ((), ())), preferred_element_type=F32)
                if mask is not None:
                    s = jnp.where(mask, s, NEG_BIG)
                m_prev = m_scr[hp]
                m_new = jnp.maximum(m_prev, jnp.max(s, axis=1, keepdims=True))
                alpha = jnp.exp(m_prev - m_new)
                p = jnp.exp(s - pltpu.repeat(m_new, w // LANES, 1)).astype(BF16)
                acc_scr[hp] = acc_scr[hp] * alpha + _dot(p, vr[h])
                m_scr[hp] = m_new
            return carry

        lax.fori_loop(0, n_trips, body, 0)

    def finalize():
        def body(hp, carry):
            for u in range(HEAD_UNROLL):
                acc = acc_scrs[u][hp]
                o_ref[hp * HEAD_UNROLL + u] = (acc / acc[:, V_HEAD_DIM:V_HEAD_DIM + 1]).astype(o_ref.dtype)
            return carry

        lax.fori_loop(0, n_trips, body, 0)

    @pl.when(j == 0)
    def _():
        for u in range(HEAD_UNROLL):
            acc_scrs[u][...] = jnp.zeros_like(acc_scrs[u])
            m_scrs[u][...] = jnp.full_like(m_scrs[u], NEG_BIG)
        step(kp_ref, vp_ref, lax.broadcasted_iota(jnp.int32, (tq, tp), 1) < prefix_valid)

    if causal:
        r = tq // tk
        @pl.when(j < r * i)
        def _():
            step(k_ref, v_ref, None)

        @pl.when((j >= r * i) & (j < r * (i + 1)))
        def _():
            qc = (i * tq + lax.broadcasted_iota(jnp.int32, (tq, tk), 0)) // CHUNK
            kc = (j * tk + lax.broadcasted_iota(jnp.int32, (tq, tk), 1)) // CHUNK
            step(k_ref, v_ref, kc <= qc)

        @pl.when(j == r * (i + 1) - 1)
        def _():
            finalize()
    else:
        mask = None
        if kv_valid is not None:
            mask = (j * tk + lax.broadcasted_iota(jnp.int32, (tq, tk), 1)) < kv_valid
        step(k_ref, v_ref, mask)

        @pl.when(j == pl.num_programs(2) - 1)
        def _():
            finalize()


def _attention(q, k, v, kp, vp, *, causal, tq, tk, prefix_valid, kv_valid=None):
    hh, b, t, _ = q.shape
    tkv = k.shape[2]
    assert t % tq == 0 and tkv % tk == 0
    if causal:
        assert tq % tk == 0 and t == tkv and tk % CHUNK == 0
        r = tq // tk
        kv_idx = lambda bi, i, j: (0, bi, jnp.minimum(r * (i + 1) - 1, j), 0)
    else:
        kv_idx = lambda bi, i, j: (0, bi, j, 0)
    tp = kp.shape[1]
    return pl.pallas_call(
        functools.partial(_attn_body, causal=causal, prefix_valid=prefix_valid, kv_valid=kv_valid),
        out_shape=jax.ShapeDtypeStruct((hh, b, t, LANES), BF16),
        grid=(b, t // tq, tkv // tk),
        in_specs=[pl.BlockSpec((hh, None, tq, LANES), lambda bi, i, j: (0, bi, i, 0)),
                  pl.BlockSpec((hh, None, tk, LANES), kv_idx),
                  pl.BlockSpec((hh, None, tk, LANES), kv_idx),
                  pl.BlockSpec((hh, tp, LANES), lambda bi, i, j: (0, 0, 0)),
                  pl.BlockSpec((hh, tp, LANES), lambda bi, i, j: (0, 0, 0))],
        out_specs=pl.BlockSpec((hh, None, tq, LANES), lambda bi, i, j: (0, bi, i, 0)),
        scratch_shapes=[pltpu.VMEM((hh // HEAD_UNROLL, tq, LANES), F32)] * (2 * HEAD_UNROLL),
        compiler_params=_cparams(3),
        name="attention",
    )(q, k, v, kp, vp)


def _mix_body(h_ref, y_ref, o_ref, gate_ref, ws_ref, wm_ref, wx_ref, out_ref):
    y_s = _dot(y_ref[...], ws_ref[...])
    y_m = _dot(o_ref[0], wm_ref[0])
    for h in range(1, MLA_HEADS):
        y_m = y_m + _dot(o_ref[h], wm_ref[h])
    merged = gate_ref[:, :D_MODEL].astype(F32) * y_s + gate_ref[:, D_MODEL:].astype(F32) * y_m
    out_ref[...] = h_ref[...] + _dot(merged.astype(BF16), wx_ref[...])


def _mix(h, y, o, gate, wts, *, tm_pref=256):
    m = h.shape[0]
    tm = _row_tile(m, tm_pref)
    return pl.pallas_call(
        _mix_body,
        out_shape=jax.ShapeDtypeStruct((m, D_MODEL), F32),
        grid=(m // tm,),
        in_specs=[pl.BlockSpec((tm, D_MODEL), lambda i: (i, 0)),
                  pl.BlockSpec((tm, D_INNER), lambda i: (i, 0)),
                  pl.BlockSpec((MLA_HEADS, tm, LANES), lambda i: (0, i, 0)),
                  pl.BlockSpec((tm, 2 * D_MODEL), lambda i: (i, 0)),
                  pl.BlockSpec((D_INNER, D_MODEL), lambda i: (0, 0)),
                  pl.BlockSpec((MLA_HEADS, LANES, D_MODEL), lambda i: (0, 0, 0)),
                  pl.BlockSpec((D_MODEL, D_MODEL), lambda i: (0, 0))],
        out_specs=pl.BlockSpec((tm, D_MODEL), lambda i: (i, 0)),
        compiler_params=_cparams(1),
        name="mix",
    )(h, y, o, gate, wts["w_ssm_out"], wts["w_mla_out"], wts["w_mix_out"])


def _prep_weights(p):
    w_in = p["w_in"]
    w_kr = w_in[:, OFF_KROPE:OFF_GATE]
    half = QK_ROPE_DIM // 2
    w_kr_sw = jnp.concatenate([w_kr[:, half:], w_kr[:, :half]], axis=1)
    zc = lambda n: jnp.zeros((D_MODEL, n), F32)
    w_misc = jnp.concatenate([
        w_in[:, OFF_CQ:OFF_CKV], w_in[:, OFF_CKV:OFF_KROPE],
        zc(QK_NOPE_DIM), w_kr, zc(LANES - QK_NOPE_DIM - QK_ROPE_DIM),
        zc(QK_NOPE_DIM), w_kr_sw, zc(LANES - QK_NOPE_DIM - QK_ROPE_DIM),
        w_in[:, OFF_DT:OFF_CQ], zc(LANES - SSM_HEADS)], axis=1)

    wq = p["w_uq"].reshape(Q_LORA, MLA_HEADS, QK_NOPE_DIM + QK_ROPE_DIM)
    q_nope, q_rope = wq[..., :QK_NOPE_DIM], wq[..., QK_NOPE_DIM:]
    q_rope_sw = jnp.concatenate([q_rope[..., half:], q_rope[..., :half]], axis=-1)
    zq = lambda n: jnp.zeros((Q_LORA, MLA_HEADS, n), F32)
    pad = LANES - QK_NOPE_DIM - QK_ROPE_DIM
    w_q1 = jnp.concatenate([q_nope, q_rope, zq(pad)], axis=-1).reshape(Q_LORA, MLA_HEADS * LANES)
    w_q2 = jnp.concatenate([zq(QK_NOPE_DIM), q_rope_sw, zq(pad)], axis=-1).reshape(Q_LORA, MLA_HEADS * LANES)

    wkv = p["w_ukv"].reshape(KV_LORA, MLA_HEADS, QK_NOPE_DIM + V_HEAD_DIM)
    zk = jnp.zeros((KV_LORA, MLA_HEADS, LANES - QK_NOPE_DIM), F32)
    w_k = jnp.concatenate([wkv[..., :QK_NOPE_DIM], zk], axis=-1).reshape(KV_LORA, MLA_HEADS * LANES)
    zv = jnp.zeros((KV_LORA, MLA_HEADS, LANES - V_HEAD_DIM), F32)
    w_v = jnp.concatenate([wkv[..., QK_NOPE_DIM:], zv], axis=-1).reshape(KV_LORA, MLA_HEADS * LANES)

    wmo = p["mla_w_out"].reshape(MLA_HEADS, V_HEAD_DIM, D_MODEL)
    w_mla_out = jnp.concatenate([wmo, jnp.zeros((MLA_HEADS, LANES - V_HEAD_DIM, D_MODEL), F32)], axis=1)

    expand = jnp.repeat(jnp.eye(SSM_HEADS, dtype=F32), SSM_HEAD_DIM, axis=1)
    one_col = (jnp.arange(LANES) == V_HEAD_DIM).astype(F32).reshape(1, LANES)
    return {
        "w_z": w_in[:, OFF_Z:OFF_XBC].astype(BF16),
        "w_xbc": w_in[:, OFF_XBC:OFF_DT].astype(BF16),
        "w_gate": w_in[:, OFF_GATE:].astype(BF16),
        "w_misc": w_misc.astype(BF16),
        "w_q1": w_q1.astype(BF16), "w_q2": w_q2.astype(BF16),
        "w_k": w_k.astype(BF16), "w_v": w_v.astype(BF16),
        "w_mla_out": w_mla_out.astype(BF16),
        "w_ssm_out": p["ssm_w_out"].astype(BF16),
        "w_mix_out": p["w_mix_out"].astype(BF16),
        "ffn1_w_gu": p["ffn1_w_gu"].astype(BF16), "ffn1_w_down": p["ffn1_w_down"].astype(BF16),
        "ffn2_w_gu": p["ffn2_w_gu"].astype(BF16), "ffn2_w_down": p["ffn2_w_down"].astype(BF16),
        "gate_bias": p["gate_bias"].reshape(1, -1),
        "conv_w": p["conv_w"], "conv_b": p["conv_b"].reshape(1, -1),
        "dt_bias": p["dt_bias"].reshape(1, -1), "a_log": p["a_log"].reshape(1, -1),
        "dvec": jnp.repeat(p["ssm_d"], SSM_HEAD_DIM).reshape(1, -1),
        "ssm_norm": p["ssm_norm"].reshape(1, -1),
        "expand": expand.astype(BF16), "one_col": one_col,
        "q_norm": p["q_norm"].reshape(1, -1), "kv_norm": p["kv_norm"].reshape(1, -1),
    }


def _rope_tables(pos):
    half = QK_ROPE_DIM // 2
    inv = ROPE_THETA ** (-jnp.arange(half, dtype=F32) / half)
    ang = pos.astype(F32)[:, None] * inv[None, :]
    c, s = jnp.cos(ang), jnp.sin(ang)
    t = pos.shape[0]
    pad = jnp.zeros((t, LANES - QK_NOPE_DIM - QK_ROPE_DIM), F32)
    cos = jnp.concatenate([jnp.ones((t, QK_NOPE_DIM), F32), c, c, pad], axis=1)
    sin = jnp.concatenate([jnp.zeros((t, QK_NOPE_DIM), F32), -s, s, pad], axis=1)
    return cos, sin


def _pad_rows(a, axis, to):
    padw = [(0, 0)] * a.ndim
    padw[axis] = (0, to - a.shape[axis])
    return jnp.pad(a, padw)


def _in_proj(u, wts):
    return (_matmul(u, wts["w_z"], BF16, act="silu"), _matmul(u, wts["w_xbc"], F32),
            _matmul(u, wts["w_gate"], BF16, bias=wts["gate_bias"], act="sigmoid"),
            _matmul(u, wts["w_misc"], F32))


def kernel(x_prompt, x_sample, cache_ckv, cache_krope, state_ssm, state_conv, meta_tokens, ffn1_norm, ffn1_w_gu, ffn1_w_down, mix_norm, w_in, gate_bias, conv_w, conv_b, dt_bias, a_log, ssm_d, ssm_norm, ssm_w_out, q_norm, w_uq, kv_norm, w_ukv, mla_w_out, w_mix_out, ffn2_norm, ffn2_w_gu, ffn2_w_down, final_norm):
    b, seq, _ = x_prompt.shape
    db, dseq, _ = x_sample.shape
    past = cache_ckv.shape[2]
    n_meta = meta_tokens.shape[0]
    nd = db * dseq
    wts = _prep_weights({
        "w_in": w_in[0], "w_uq": w_uq[0], "w_ukv": w_ukv[0], "mla_w_out": mla_w_out[0],
        "ssm_w_out": ssm_w_out[0], "w_mix_out": w_mix_out[0],
        "ffn1_w_gu": ffn1_w_gu[0], "ffn1_w_down": ffn1_w_down[0],
        "ffn2_w_gu": ffn2_w_gu[0], "ffn2_w_down": ffn2_w_down[0],
        "gate_bias": gate_bias[0], "conv_w": conv_w[0], "conv_b": conv_b[0], "dt_bias": dt_bias[0],
        "a_log": a_log[0], "ssm_d": ssm_d[0], "ssm_norm": ssm_norm[0],
        "q_norm": q_norm[0], "kv_norm": kv_norm[0]})

    xb = x_prompt.reshape(b * seq, D_MODEL)
    xs = jnp.concatenate([x_sample.reshape(nd, D_MODEL), meta_tokens.astype(F32)], axis=0)

    hb, ub = _ffn(xb, ffn1_norm[0], wts["ffn1_w_gu"], wts["ffn1_w_down"], mix_norm[0], emit_h=True, out_dtype=BF16)
    hs, us = _ffn(xs, ffn1_norm[0], wts["ffn1_w_gu"], wts["ffn1_w_down"], mix_norm[0], emit_h=True, out_dtype=BF16)
    zb, xbcb, gateb, miscb = _in_proj(ub, wts)
    zs, xbcs, gates, miscs = _in_proj(us, wts)

    L = SSD_L
    meta3 = lambda a: _pad_rows(a[nd:][None], 1, L)
    _, ssm_meta = _ssd(meta3(xbcs), meta3(miscs), meta3(zs),
                       jnp.zeros((1, CONV_WIDTH - 1, CONV_DIM), F32),
                       jnp.zeros((1, SSM_HEADS, SSM_HEAD_DIM, SSM_STATE), F32), wts, t_valid=n_meta)
    conv_meta = xbcs[nd + n_meta - (CONV_WIDTH - 1):nd + n_meta][None]
    yb, ssm_p = _ssd(xbcb.reshape(b, seq, -1), miscb.reshape(b, seq, -1), zb.reshape(b, seq, -1),
                     jnp.broadcast_to(conv_meta, (b,) + conv_meta.shape[1:]),
                     jnp.broadcast_to(ssm_meta, (b,) + ssm_meta.shape[1:]), wts, t_valid=None)
    dec3 = lambda a: _pad_rows(a[:nd].reshape(db, dseq, -1), 1, L)
    yd, ssm_s = _ssd(dec3(xbcs), dec3(miscs), dec3(zs), state_conv[0], state_ssm[0], wts, t_valid=dseq)
    yd = yd[:, :dseq].reshape(nd, D_INNER)
    conv_p = xbcb.reshape(b, seq, -1)[:, seq - (CONV_WIDTH - 1):]
    xbcd = xbcs[:nd].reshape(db, dseq, -1)
    conv_s = jnp.concatenate([state_conv[0].astype(F32), xbcd], axis=1)[:, -(CONV_WIDTH - 1):]

    tq = _row_tile(seq, 512)
    cos_b, sin_b = _rope_tables(jnp.arange(seq, dtype=jnp.int32))
    pos_s = jnp.concatenate([jnp.tile(past + jnp.arange(dseq, dtype=jnp.int32), db),
                             jnp.arange(n_meta, dtype=jnp.int32) - n_meta])
    cos_s, sin_s = _rope_tables(pos_s)
    qb, ckvb, krb = _mla_prep(miscb, cos_b, sin_b, seq // tq, wts, tm=tq)
    qs, ckvs, krs = _mla_prep(miscs, cos_s, sin_s, 1, wts, tm=miscs.shape[0])
    kb, vb = _kv_expand(ckvb, krb, wts, tm=tq)
    kp, vp = _kv_expand(_pad_rows(ckvs[nd:], 0, LANES), _pad_rows(krs[nd:], 0, LANES), wts, tm=LANES)
    tkd = -(-(past + dseq) // LANES) * LANES
    ckv_d = _pad_rows(jnp.concatenate([cache_ckv[0].astype(F32), ckvs[:nd].reshape(db, dseq, -1)], axis=1), 1, tkd)
    kr_cache = jnp.pad(cache_krope[0].astype(F32),
                       ((0, 0), (0, 0), (QK_NOPE_DIM, LANES - QK_NOPE_DIM - QK_ROPE_DIM)))
    kr_d = _pad_rows(jnp.concatenate([kr_cache, krs[:nd].reshape(db, dseq, -1)], axis=1), 1, tkd)
    kd, vd = _kv_expand(ckv_d.reshape(db * tkd, -1), kr_d.reshape(db * tkd, -1), wts,
                        tm=_row_tile(db * tkd, 512))

    hsplit = lambda a, nb: a.reshape(MLA_HEADS, nb, -1, LANES)
    ob = _attention(hsplit(qb, b), hsplit(kb, b), hsplit(vb, b), kp, vp,
                    causal=True, tq=_row_tile(seq, ATTN_TQ), tk=tq, prefix_valid=n_meta)
    od = _attention(hsplit(qs[:, :nd], db), hsplit(kd, db), hsplit(vd, db), kp, vp,
                    causal=False, tq=dseq, tk=tkd, prefix_valid=n_meta, kv_valid=past + dseq)

    h2b = _mix(hb, yb.reshape(b * seq, -1), ob.reshape(MLA_HEADS, b * seq, LANES), gateb, wts)
    h2d = _mix(hs[:nd], yd, od.reshape(MLA_HEADS, nd, LANES), gates[:nd], wts)
    (y_p,) = _ffn(h2b, ffn2_norm[0], wts["ffn2_w_gu"], wts["ffn2_w_down"], final_norm, emit_h=False, out_dtype=F32)
    (y_s,) = _ffn(h2d, ffn2_norm[0], wts["ffn2_w_gu"], wts["ffn2_w_down"], final_norm, emit_h=False, out_dtype=F32)

    kr32 = lambda a: a[..., QK_NOPE_DIM:QK_NOPE_DIM + QK_ROPE_DIM]
    ckv_meta = jnp.broadcast_to(ckvs[nd:][None], (b, n_meta, KV_LORA))
    kr_meta = jnp.broadcast_to(kr32(krs[nd:])[None], (b, n_meta, QK_ROPE_DIM))
    ckv_p = jnp.concatenate([ckv_meta, ckvb.reshape(b, seq, -1)], axis=1)
    kr_p = jnp.concatenate([kr_meta, kr32(krb).reshape(b, seq, -1)], axis=1)
    return (y_p.reshape(b, seq, D_MODEL), y_s.reshape(db, dseq, D_MODEL),
            ckv_p[None], kr_p[None], ssm_p[None], conv_p[None],
            ckvs[:nd].reshape(db, dseq, -1)[None], kr32(krs[:nd]).reshape(db, dseq, -1)[None],
            ssm_s[None], conv_s[None])
```

```python
import functools
import math

import jax
import jax.numpy as jnp
from jax import lax
from jax.experimental import pallas as pl
from jax.experimental.pallas import tpu as pltpu

F32 = jnp.float32
BF16 = jnp.bfloat16

D_MODEL = 1024
CHUNK = 64
FFN_DIM = 2816
NORM_EPS = 1e-6
D_INNER = 2048
SSM_HEAD_DIM = 64
SSM_HEADS = 32
SSM_GROUPS = 4
SSM_STATE = 128
CONV_WIDTH = 4
CONV_DIM = D_INNER + 2 * SSM_GROUPS * SSM_STATE
MLA_HEADS = 16
HEAD_PAIRS = MLA_HEADS // 2
Q_LORA = 384
KV_LORA = 256
QK_NOPE_DIM = 64
QK_ROPE_DIM = 32
V_HEAD_DIM = 64
ROPE_THETA = 10000.0
OFF_Z = 0
OFF_XBC = OFF_Z + D_INNER
OFF_DT = OFF_XBC + CONV_DIM
OFF_CQ = OFF_DT + SSM_HEADS
OFF_CKV = OFF_CQ + Q_LORA
OFF_KROPE = OFF_CKV + KV_LORA
OFF_GATE = OFF_KROPE + QK_ROPE_DIM

LANES = 128
SSD_L = 128
FFN_TF = 256
HEAD_UNROLL = 8
ATTN_TQ = 1024
NEG_BIG = -1e30
VMEM_LIMIT = 52 * 1024 * 1024

MISC_CQ = 0
MISC_CKV = 384
MISC_KRA = 640
MISC_KRB = 768
MISC_DT = 896
MISC_W = 1024


def _cparams(n_grid):
    return pltpu.CompilerParams(dimension_semantics=("arbitrary",) * n_grid,
                                vmem_limit_bytes=VMEM_LIMIT)


def _row_tile(m, pref):
    if m <= pref:
        return m
    t = pref
    while t >= 16:
        if m % t == 0:
            return t
        t //= 2
    return m


def _rms(x, g):
    return x * lax.rsqrt(jnp.mean(x * x, axis=-1, keepdims=True) + NORM_EPS) * g


def _dot(a, b):
    return jnp.dot(a, b, preferred_element_type=F32)


def _ffn_body(x_ref, g1_ref, wgu_ref, wd_ref, g2_ref, *outs, emit_h):
    x = x_ref[...]
    xn = _rms(x, g1_ref[...]).astype(BF16)
    acc = None
    for f in range(0, FFN_DIM, FFN_TF):
        g = _dot(xn, wgu_ref[:, f:f + FFN_TF])
        u = _dot(xn, wgu_ref[:, FFN_DIM + f:FFN_DIM + f + FFN_TF])
        act = (g * jax.nn.sigmoid(g) * u).astype(BF16)
        d = _dot(act, wd_ref[f:f + FFN_TF, :])
        acc = d if acc is None else acc + d
    h = x + 0.5 * acc
    if emit_h:
        outs[0][...] = h
    outs[-1][...] = _rms(h, g2_ref[...]).astype(outs[-1].dtype)


def _ffn(x, g1, w_gu, w_down, g2, *, emit_h, out_dtype, tm_pref=512):
    m = x.shape[0]
    tm = _row_tile(m, tm_pref)
    row = pl.BlockSpec((tm, D_MODEL), lambda i: (i, 0))
    vec = pl.BlockSpec((1, D_MODEL), lambda i: (0, 0))
    out_shape = [jax.ShapeDtypeStruct((m, D_MODEL), out_dtype)]
    out_specs = [row]
    if emit_h:
        out_shape = [jax.ShapeDtypeStruct((m, D_MODEL), F32)] + out_shape
        out_specs = [row, row]
    return pl.pallas_call(
        functools.partial(_ffn_body, emit_h=emit_h),
        out_shape=out_shape,
        grid=(m // tm,),
        in_specs=[row, vec,
                  pl.BlockSpec((D_MODEL, 2 * FFN_DIM), lambda i: (0, 0), pipeline_mode=pl.Buffered(1)),
                  pl.BlockSpec((FFN_DIM, D_MODEL), lambda i: (0, 0), pipeline_mode=pl.Buffered(1)),
                  vec],
        out_specs=out_specs,
        compiler_params=_cparams(1),
        name="ffn",
    )(x, g1.reshape(1, -1), w_gu, w_down, g2.reshape(1, -1))


def _softplus(x):
    return jnp.maximum(x, 0.0) + jnp.log1p(jnp.exp(-jnp.abs(x)))


def _expand_heads(x, e):
    hi = x.astype(BF16)
    lo = (x - hi.astype(F32)).astype(BF16)
    return _dot(hi, e) + _dot(lo, e)


def _ssd_body(xbc_ref, dtb_ref, zs_ref, convp_ref, ssmp_ref, cw_ref, cb_ref, dtbias_ref, alog_ref,
              dvec_ref, nrm_ref, e_ref, y_ref, ssm_ref, xpad, state, y_scr, *, t_valid):
    t = pl.program_id(1)
    L = SSD_L
    gs = SSM_STATE
    hg = SSM_HEADS // SSM_GROUPS
    gw = hg * SSM_HEAD_DIM

    @pl.when(t == 0)
    def _():
        xpad[0:8, :] = convp_ref[...]
        state[...] = ssmp_ref[...].T

    xpad[8:8 + L, :] = xbc_ref[...]

    def conv_act(c0, width):
        cs = slice(c0, c0 + width)
        acc = cb_ref[:, cs] + xpad[5:5 + L, cs] * cw_ref[0:1, cs]
        for k in range(1, CONV_WIDTH):
            acc = acc + xpad[5 + k:5 + k + L, cs] * cw_ref[k:k + 1, cs]
        return acc * jax.nn.sigmoid(acc)

    dt =_softplus(dtb_ref[:, 0:SSM_HEADS] + dtbias_ref[...])
    if t_valid is not None:
        rows = t * L + lax.broadcasted_iota(jnp.int32, (L, SSM_HEADS), 0)
        dt = jnp.where(rows < t_valid, dt, 0.0)
    da = dt * (-jnp.exp(alog_ref[...]))
    ri = lax.broadcasted_iota(jnp.int32, (L, L), 0)
    ci = lax.broadcasted_iota(jnp.int32, (L, L), 1)
    causal = ri >= ci
    a_cs = jnp.dot(causal.astype(F32), da, precision=lax.Precision.HIGHEST,
                   preferred_element_type=F32)
    eye = (lax.broadcasted_iota(jnp.int32, (SSM_HEADS, SSM_HEADS), 0)
           == lax.broadcasted_iota(jnp.int32, (SSM_HEADS, SSM_HEADS), 1)).astype(F32)
    a_cs_t = lax.dot_general(eye, a_cs, (((1,), (1,)), ((), ())), precision=lax.Precision.HIGHEST,
                             preferred_element_type=F32)
    total = a_cs[L - 1:L, :]

    exp_cs = jnp.exp(a_cs)
    w_in = jnp.exp(total - a_cs) * dt
    exp_tot = jnp.broadcast_to(jnp.exp(total), (8, SSM_HEADS))

    for g in range(SSM_GROUPS):
        gsl = slice(g * gw, (g + 1) * gw)
        xs = conv_act(g * gw, gw)
        bg = conv_act(D_INNER + g * gs, gs).astype(BF16)
        cg = conv_act(D_INNER + SSM_GROUPS * gs + g * gs, gs).astype(BF16)
        e = e_ref[:, gsl]
        xdt = xs * _expand_heads(dt, e)
        wx = (xs * _expand_heads(w_in, e)).astype(BF16)
        cb = lax.dot_general(cg, bg, (((1,), (1,)), ((), ())), preferred_element_type=F32)
        cb = jnp.where(causal, cb, 0.0)
        for hh in range(hg):
            h = g * hg + hh
            seg = a_cs[:, h:h + 1] - a_cs_t[h:h + 1, :]
            mh = (cb * jnp.exp(jnp.minimum(seg, 0.0))).astype(BF16)
            cs = slice(hh * SSM_HEAD_DIM, (hh + 1) * SSM_HEAD_DIM)
            y_scr[:, cs] = _dot(mh, xdt[:, cs].astype(BF16))
        st = state[:, gsl]
        y = y_scr[...] + _dot(cg, st.astype(BF16)) * _expand_heads(exp_cs, e) + dvec_ref[:, gsl] * xs
        upd = lax.dot_general(bg, wx, (((0,), (0,)), ((), ())), preferred_element_type=F32)
        state[:, gsl] = st * _expand_heads(exp_tot, e)[0:1, :] + upd
        y = y * zs_ref[:, gsl].astype(F32)
        y = y * lax.rsqrt(jnp.mean(y * y, axis=-1, keepdims=True) + NORM_EPS)
        y_ref[:, gsl] = (y * nrm_ref[:, gsl]).astype(y_ref.dtype)

    xpad[0:8, :] = xpad[L:L + 8, :]

    @pl.when(t == pl.num_programs(1) - 1)
    def _():
        ssm_ref[...] = state[...].T


def _ssd(xbc, misc, z, conv_prev, ssm_prev, wts, *, t_valid):
    b, t, _ = xbc.shape
    assert t % SSD_L == 0
    convp = jnp.pad(conv_prev.astype(F32), ((0, 0), (8 - (CONV_WIDTH - 1), 0), (0, 0)))
    ssmp = ssm_prev.astype(F32).reshape(b, D_INNER, SSM_STATE)
    tok = lambda w: pl.BlockSpec((None, SSD_L, w), lambda bi, ti: (bi, ti, 0))
    full = lambda r, c: pl.BlockSpec((r, c), lambda bi, ti: (0, 0))
    y, ssm = pl.pallas_call(
        functools.partial(_ssd_body, t_valid=t_valid),
        out_shape=[jax.ShapeDtypeStruct((b, t, D_INNER), BF16),
                   jax.ShapeDtypeStruct((b, D_INNER, SSM_STATE), F32)],
        grid=(b, t // SSD_L),
        in_specs=[tok(CONV_DIM),
                  pl.BlockSpec((None, SSD_L, LANES), lambda bi, ti: (bi, ti, MISC_DT // LANES)),
                  tok(D_INNER),
                  pl.BlockSpec((None, 8, CONV_DIM), lambda bi, ti: (bi, 0, 0)),
                  pl.BlockSpec((None, D_INNER, SSM_STATE), lambda bi, ti: (bi, 0, 0)),
                  full(CONV_WIDTH, CONV_DIM), full(1, CONV_DIM), full(1, SSM_HEADS), full(1, SSM_HEADS),
                  full(1, D_INNER), full(1, D_INNER), full(SSM_HEADS, D_INNER)],
        out_specs=[tok(D_INNER),
                   pl.BlockSpec((None, D_INNER, SSM_STATE), lambda bi, ti: (bi, 0, 0))],
        scratch_shapes=[pltpu.VMEM((SSD_L + 8, CONV_DIM), F32),
                        pltpu.VMEM((SSM_STATE, D_INNER), F32),
                        pltpu.VMEM((SSD_L, D_INNER // SSM_GROUPS), F32)],
        compiler_params=_cparams(2),
        name="ssd",
    )(xbc, misc, z, convp, ssmp, wts["conv_w"], wts["conv_b"], wts["dt_bias"], wts["a_log"],
      wts["dvec"], wts["ssm_norm"], wts["expand"])
    return y, ssm.reshape(b, SSM_HEADS, SSM_HEAD_DIM, SSM_STATE)


def _mla_prep_body(misc_ref, cos_ref, sin_ref, qn_ref, kvn_ref, w1_ref, w2_ref, q_ref, ckv_ref, kr_ref):
    cos = cos_ref[...]
    sin = sin_ref[...]
    qn = _rms(misc_ref[:, MISC_CQ:MISC_CQ + Q_LORA], qn_ref[...]).astype(BF16)
    q1 = _dot(qn, w1_ref[...])
    q2 = _dot(qn, w2_ref[...])
    scale = (QK_NOPE_DIM + QK_ROPE_DIM) ** -0.5
    for h in range(MLA_HEADS):
        hs = slice(h * LANES, (h + 1) * LANES)
        q_ref[h] = ((q1[:, hs] * cos + q2[:, hs] * sin) * scale).astype(BF16)
    ckv_ref[...] = _rms(misc_ref[:, MISC_CKV:MISC_CKV + KV_LORA], kvn_ref[...])
    kr_ref[...] = misc_ref[:, MISC_KRA:MISC_KRA + LANES] * cos + misc_ref[:, MISC_KRB:MISC_KRB + LANES] * sin


def _mla_prep(misc, cos, sin, n_pos_blocks, wts, *, tm):
    m = misc.shape[0]
    assert m % tm == 0
    return pl.pallas_call(
        _mla_prep_body,
        out_shape=[jax.ShapeDtypeStruct((MLA_HEADS, m, LANES), BF16),
                   jax.ShapeDtypeStruct((m, KV_LORA), F32),
                   jax.ShapeDtypeStruct((m, LANES), F32)],
        grid=(m // tm,),
        in_specs=[pl.BlockSpec((tm, MISC_W), lambda i: (i, 0)),
                  pl.BlockSpec((tm, LANES), lambda i: (i % n_pos_blocks, 0)),
                  pl.BlockSpec((tm, LANES), lambda i: (i % n_pos_blocks, 0)),
                  pl.BlockSpec((1, Q_LORA), lambda i: (0, 0)),
                  pl.BlockSpec((1, KV_LORA), lambda i: (0, 0)),
                  pl.BlockSpec((Q_LORA, MLA_HEADS * LANES), lambda i: (0, 0)),
                  pl.BlockSpec((Q_LORA, MLA_HEADS * LANES), lambda i: (0, 0))],
        out_specs=[pl.BlockSpec((MLA_HEADS, tm, LANES), lambda i: (0, i, 0)),
                   pl.BlockSpec((tm, KV_LORA), lambda i: (i, 0)),
                   pl.BlockSpec((tm, LANES), lambda i: (i, 0))],
        compiler_params=_cparams(1),
        name="mla_prep",
    )(misc, cos, sin, wts["q_norm"], wts["kv_norm"], wts["w_q1"], wts["w_q2"])


def _kv_body(ckv_ref, kr_ref, wk_ref, wv_ref, one_ref, k_ref, v_ref):
    c = ckv_ref[...].astype(BF16)
    k = _dot(c, wk_ref[...])
    v = _dot(c, wv_ref[...])
    kr = kr_ref[...]
    one = one_ref[...]
    for h in range(MLA_HEADS):
        hs = slice(h * LANES, (h + 1) * LANES)
        k_ref[h] = (k[:, hs] + kr).astype(BF16)
        v_ref[h] = (v[:, hs] + one).astype(BF16)


def _kv_expand(ckv, kr128, wts, *, tm):
    m = ckv.shape[0]
    assert m % tm == 0
    hw = MLA_HEADS * LANES
    return pl.pallas_call(
        _kv_body,
        out_shape=[jax.ShapeDtypeStruct((MLA_HEADS, m, LANES), BF16)] * 2,
        grid=(m // tm,),
        in_specs=[pl.BlockSpec((tm, KV_LORA), lambda i: (i, 0)),
                  pl.BlockSpec((tm, LANES), lambda i: (i, 0)),
                  pl.BlockSpec((KV_LORA, hw), lambda i: (0, 0)),
                  pl.BlockSpec((KV_LORA, hw), lambda i: (0, 0)),
                  pl.BlockSpec((1, LANES), lambda i: (0, 0))],
        out_specs=[pl.BlockSpec((MLA_HEADS, tm, LANES), lambda i: (0, i, 0))] * 2,
        compiler_params=_cparams(1),
        name="kv_expand",
    )(ckv, kr128, wts["w_k"], wts["w_v"], wts["one_col"])


def _attn_body(q_ref, k_ref, v_ref, kp_ref, vp_ref, o_ref, *scr, causal, prefix_valid, kv_valid):
    acc_scrs, m_scrs = scr[:HEAD_UNROLL], scr[HEAD_UNROLL:]
    i = pl.program_id(1)
    j = pl.program_id(2)
    tq = q_ref.shape[1]
    tk = k_ref.shape[1]
    tp = kp_ref.shape[1]
    n_trips = MLA_HEADS // HEAD_UNROLL

    def step(kr, vr, mask, row0=0):
        w = kr.shape[1]
        rows = pl.ds(row0, tq - row0)

        def body(hp, carry):
            for u in range(HEAD_UNROLL):
                h = hp * HEAD_UNROLL + u
                acc_scr, m_scr = acc_scrs[u], m_scrs[u]
                s = lax.dot_general(q_ref[h, rows, :], kr[h], (((1,), (1,)), ((), ())),
                                    preferred_element_type=F32)
                if mask is not None:
                    s = jnp.where(mask, s, NEG_BIG)
                m_prev = m_scr[hp, rows, :]
                m_new = jnp.maximum(m_prev, jnp.max(s, axis=1, keepdims=True))
                alpha = jnp.exp(m_prev - m_new)
                p = jnp.exp(s - pltpu.repeat(m_new, w // LANES, 1)).astype(BF16)
                acc_scr[hp, rows, :] = acc_scr[hp, rows, :] * alpha + _dot(p, vr[h])
                m_scr[hp, rows, :] = m_new
            return carry

        lax.fori_loop(0, n_trips, body, 0)

    def finalize():
        lane = lax.broadcasted_iota(jnp.int32, (tq, LANES), 1)

        def body(hp, carry):
            for u in range(0, HEAD_UNROLL, 2):
                a = acc_scrs[u][hp]
                b = acc_scrs[u + 1][hp]
                oa = a / a[:, V_HEAD_DIM:V_HEAD_DIM + 1]
                ob = b / b[:, V_HEAD_DIM:V_HEAD_DIM + 1]
                pair = jnp.where(lane < V_HEAD_DIM, oa, pltpu.roll(ob, V_HEAD_DIM, 1))
                o_ref[hp * (HEAD_UNROLL // 2) + u // 2] = pair.astype(o_ref.dtype)
            return carry

        lax.fori_loop(0, n_trips, body, 0)

    @pl.when(j == 0)
    def _():
        for u in range(HEAD_UNROLL):
            acc_scrs[u][...] = jnp.zeros_like(acc_scrs[u])
            m_scrs[u][...] = jnp.full_like(m_scrs[u], NEG_BIG)
        step(kp_ref, vp_ref, lax.broadcasted_iota(jnp.int32, (tq, tp), 1) < prefix_valid)

    if causal:
        r = tq // tk
        @pl.when(j < r * i)
        def _():
            step(k_ref, v_ref, None)

        for jj in range(r):
            @pl.when(j == r * i + jj)
            def _(jj=jj):
                row0 = jj * tk
                qc = (i * tq + row0 + lax.broadcasted_iota(jnp.int32, (tq - row0, tk), 0)) // CHUNK
                kc = (j * tk + lax.broadcasted_iota(jnp.int32, (tq - row0, tk), 1)) // CHUNK
                step(k_ref, v_ref, kc <= qc, row0)

        @pl.when(j == r * (i + 1) - 1)
        def _():
            finalize()
    else:
        mask = None
        if kv_valid is not None:
            mask = (j * tk + lax.broadcasted_iota(jnp.int32, (tq, tk), 1)) < kv_valid
        step(k_ref, v_ref, mask)

        @pl.when(j == pl.num_programs(2) - 1)
        def _():
            finalize()


def _attention(q, k, v, kp, vp, *, causal, tq, tk, prefix_valid, kv_valid=None):
    hh, b, t, _ = q.shape
    tkv = k.shape[2]
    assert t % tq == 0 and tkv % tk == 0
    if causal:
        assert tq % tk == 0 and t == tkv and tk % CHUNK == 0
        r = tq // tk
        kv_idx = lambda bi, i, j: (0, bi, jnp.minimum(r * (i + 1) - 1, j), 0)
    else:
        kv_idx = lambda bi, i, j: (0, bi, j, 0)
    tp = kp.shape[1]
    return pl.pallas_call(
        functools.partial(_attn_body, causal=causal, prefix_valid=prefix_valid, kv_valid=kv_valid),
        out_shape=jax.ShapeDtypeStruct((hh // 2, b, t, LANES), BF16),
        grid=(b, t // tq, tkv // tk),
        in_specs=[pl.BlockSpec((hh, None, tq, LANES), lambda bi, i, j: (0, bi, i, 0)),
                  pl.BlockSpec((hh, None, tk, LANES), kv_idx),
                  pl.BlockSpec((hh, None, tk, LANES), kv_idx),
                  pl.BlockSpec((hh, tp, LANES), lambda bi, i, j: (0, 0, 0)),
                  pl.BlockSpec((hh, tp, LANES), lambda bi, i, j: (0, 0, 0))],
        out_specs=pl.BlockSpec((hh // 2, None, tq, LANES), lambda bi, i, j: (0, bi, i, 0)),
        scratch_shapes=[pltpu.VMEM((hh // HEAD_UNROLL, tq, LANES), F32)] * (2 * HEAD_UNROLL),
        compiler_params=_cparams(3),
        name="attention",
    )(q, k, v, kp, vp)


def _mix_body(h_ref, y_ref, o_ref, gate_ref, ws_ref, wm_ref, wx_ref, out_ref):
    y_s = _dot(y_ref[...], ws_ref[...])
    y_m = _dot(o_ref[0], wm_ref[0])
    for hp in range(1, HEAD_PAIRS):
        y_m = y_m + _dot(o_ref[hp], wm_ref[hp])
    merged = gate_ref[:, :D_MODEL].astype(F32) * y_s + gate_ref[:, D_MODEL:].astype(F32) * y_m
    out_ref[...] = h_ref[...] + _dot(merged.astype(BF16), wx_ref[...])


def _mix(h, y, o, gate, wts, *, tm_pref=256):
    m = h.shape[0]
    tm = _row_tile(m, tm_pref)
    return pl.pallas_call(
        _mix_body,
        out_shape=jax.ShapeDtypeStruct((m, D_MODEL), F32),
        grid=(m // tm,),
        in_specs=[pl.BlockSpec((tm, D_MODEL), lambda i: (i, 0)),
                  pl.BlockSpec((tm, D_INNER), lambda i: (i, 0)),
                  pl.BlockSpec((HEAD_PAIRS, tm, LANES), lambda i: (0, i, 0)),
                  pl.BlockSpec((tm, 2 * D_MODEL), lambda i: (i, 0)),
                  pl.BlockSpec((D_INNER, D_MODEL), lambda i: (0, 0)),
                  pl.BlockSpec((HEAD_PAIRS, LANES, D_MODEL), lambda i: (0, 0, 0)),
                  pl.BlockSpec((D_MODEL, D_MODEL), lambda i: (0, 0))],
        out_specs=pl.BlockSpec((tm, D_MODEL), lambda i: (i, 0)),
        compiler_params=_cparams(1),
        name="mix",
    )(h, y, o, gate, wts["w_ssm_out"], wts["w_mla_out"], wts["w_mix_out"])


def _prep_weights(p):
    w_in = p["w_in"]
    w_kr = w_in[:, OFF_KROPE:OFF_GATE]
    half = QK_ROPE_DIM // 2
    w_kr_sw = jnp.concatenate([w_kr[:, half:], w_kr[:, :half]], axis=1)
    zc = lambda n: jnp.zeros((D_MODEL, n), F32)
    w_misc = jnp.concatenate([
        w_in[:, OFF_CQ:OFF_CKV], w_in[:, OFF_CKV:OFF_KROPE],
        zc(QK_NOPE_DIM), w_kr, zc(LANES - QK_NOPE_DIM - QK_ROPE_DIM),
        zc(QK_NOPE_DIM), w_kr_sw, zc(LANES - QK_NOPE_DIM - QK_ROPE_DIM),
        w_in[:, OFF_DT:OFF_CQ], zc(LANES - SSM_HEADS)], axis=1)

    wq = p["w_uq"].reshape(Q_LORA, MLA_HEADS, QK_NOPE_DIM + QK_ROPE_DIM)
    q_nope, q_rope = wq[..., :QK_NOPE_DIM], wq[..., QK_NOPE_DIM:]
    q_rope_sw = jnp.concatenate([q_rope[..., half:], q_rope[..., :half]], axis=-1)
    zq = lambda n: jnp.zeros((Q_LORA, MLA_HEADS, n), F32)
    pad = LANES - QK_NOPE_DIM - QK_ROPE_DIM
    w_q1 = jnp.concatenate([q_nope, q_rope, zq(pad)], axis=-1).reshape(Q_LORA, MLA_HEADS * LANES)
    w_q2 = jnp.concatenate([zq(QK_NOPE_DIM), q_rope_sw, zq(pad)], axis=-1).reshape(Q_LORA, MLA_HEADS * LANES)

    wkv = p["w_ukv"].reshape(KV_LORA, MLA_HEADS, QK_NOPE_DIM + V_HEAD_DIM)
    zk = jnp.zeros((KV_LORA, MLA_HEADS, LANES - QK_NOPE_DIM), F32)
    w_k = jnp.concatenate([wkv[..., :QK_NOPE_DIM], zk], axis=-1).reshape(KV_LORA, MLA_HEADS * LANES)
    zv = jnp.zeros((KV_LORA, MLA_HEADS, LANES - V_HEAD_DIM), F32)
    w_v = jnp.concatenate([wkv[..., QK_NOPE_DIM:], zv], axis=-1).reshape(KV_LORA, MLA_HEADS * LANES)

    w_mla_out = p["mla_w_out"].reshape(HEAD_PAIRS, 2 * V_HEAD_DIM, D_MODEL)

    expand = jnp.repeat(jnp.eye(SSM_HEADS, dtype=F32), SSM_HEAD_DIM, axis=1)
    one_col = (jnp.arange(LANES) == V_HEAD_DIM).astype(F32).reshape(1, LANES)
    return {
        "w_z": w_in[:, OFF_Z:OFF_XBC].astype(BF16),
        "w_xbc": w_in[:, OFF_XBC:OFF_DT].astype(BF16),
        "w_gate": w_in[:, OFF_GATE:].astype(BF16),
        "w_misc": w_misc.astype(BF16),
        "w_q1": w_q1.astype(BF16), "w_q2": w_q2.astype(BF16),
        "w_k": w_k.astype(BF16), "w_v": w_v.astype(BF16),
        "w_mla_out": w_mla_out.astype(BF16),
        "w_ssm_out": p["ssm_w_out"].astype(BF16),
        "w_mix_out": p["w_mix_out"].astype(BF16),
        "ffn1_w_gu": p["ffn1_w_gu"].astype(BF16), "ffn1_w_down": p["ffn1_w_down"].astype(BF16),
        "ffn2_w_gu": p["ffn2_w_gu"].astype(BF16), "ffn2_w_down": p["ffn2_w_down"].astype(BF16),
        "gate_bias": p["gate_bias"].reshape(1, -1),
        "conv_w": p["conv_w"], "conv_b": p["conv_b"].reshape(1, -1),
        "dt_bias": p["dt_bias"].reshape(1, -1), "a_log": p["a_log"].reshape(1, -1),
        "dvec": jnp.repeat(p["ssm_d"], SSM_HEAD_DIM).reshape(1, -1),
        "ssm_norm": p["ssm_norm"].reshape(1, -1),
        "expand": expand.astype(BF16), "one_col": one_col,
        "q_norm": p["q_norm"].reshape(1, -1), "kv_norm": p["kv_norm"].reshape(1, -1),
    }


def _rope_tables(pos):
    half = QK_ROPE_DIM // 2
    inv = ROPE_THETA ** (-jnp.arange(half, dtype=F32) / half)
    ang = pos.astype(F32)[:, None] * inv[None, :]
    c, s = jnp.cos(ang), jnp.sin(ang)
    t = pos.shape[0]
    pad = jnp.zeros((t, LANES - QK_NOPE_DIM - QK_ROPE_DIM), F32)
    cos = jnp.concatenate([jnp.ones((t, QK_NOPE_DIM), F32), c, c, pad], axis=1)
    sin = jnp.concatenate([jnp.zeros((t, QK_NOPE_DIM), F32), -s, s, pad], axis=1)
    return cos, sin


def _pad_rows(a, axis, to):
    padw = [(0, 0)] * a.ndim
    padw[axis] = (0, to - a.shape[axis])
    return jnp.pad(a, padw)


PROJ_CW = 512


def _in_proj_body(u_ref, wz_ref, wx_ref, wg_ref, wm_ref, gb_ref, zs_ref, xbc_ref, gate_ref, misc_ref):
    x = u_ref[...]
    for c in range(0, D_INNER, PROJ_CW):
        cs = slice(c, c + PROJ_CW)
        z = _dot(x, wz_ref[:, cs])
        zs_ref[:, cs] = (z * jax.nn.sigmoid(z)).astype(zs_ref.dtype)
    for c in range(0, CONV_DIM, PROJ_CW):
        cs = slice(c, c + PROJ_CW)
        xbc_ref[:, cs] = _dot(x, wx_ref[:, cs])
    for c in range(0, 2 * D_MODEL, PROJ_CW):
        cs = slice(c, c + PROJ_CW)
        gate_ref[:, cs] = jax.nn.sigmoid(_dot(x, wg_ref[:, cs]) + gb_ref[:, cs]).astype(gate_ref.dtype)
    for c in range(0, MISC_W, PROJ_CW):
        cs = slice(c, c + PROJ_CW)
        misc_ref[:, cs] = _dot(x, wm_ref[:, cs])


def _in_proj(u, wts, *, tm_pref=512):
    m = u.shape[0]
    tm = _row_tile(m, tm_pref)
    widths = (D_INNER, CONV_DIM, 2 * D_MODEL, MISC_W)
    dtypes = (BF16, F32, BF16, F32)
    resident = lambda n: pl.BlockSpec((D_MODEL, n), lambda i: (0, 0), pipeline_mode=pl.Buffered(1))
    return pl.pallas_call(
        _in_proj_body,
        out_shape=[jax.ShapeDtypeStruct((m, n), dt) for n, dt in zip(widths, dtypes)],
        grid=(m // tm,),
        in_specs=[pl.BlockSpec((tm, D_MODEL), lambda i: (i, 0))] + [resident(n) for n in widths]
                 + [pl.BlockSpec((1, 2 * D_MODEL), lambda i: (0, 0))],
        out_specs=[pl.BlockSpec((tm, n), lambda i: (i, 0)) for n in widths],
        compiler_params=_cparams(1),
        name="in_proj",
    )(u, wts["w_z"], wts["w_xbc"], wts["w_gate"], wts["w_misc"], wts["gate_bias"])


def kernel(x_prompt, x_sample, cache_ckv, cache_krope, state_ssm, state_conv, meta_tokens, ffn1_norm, ffn1_w_gu, ffn1_w_down, mix_norm, w_in, gate_bias, conv_w, conv_b, dt_bias, a_log, ssm_d, ssm_norm, ssm_w_out, q_norm, w_uq, kv_norm, w_ukv, mla_w_out, w_mix_out, ffn2_norm, ffn2_w_gu, ffn2_w_down, final_norm):
    b, seq, _ = x_prompt.shape
    db, dseq, _ = x_sample.shape
    past = cache_ckv.shape[2]
    n_meta = meta_tokens.shape[0]
    nd = db * dseq
    wts = _prep_weights({
        "w_in": w_in[0], "w_uq": w_uq[0], "w_ukv": w_ukv[0], "mla_w_out": mla_w_out[0],
        "ssm_w_out": ssm_w_out[0], "w_mix_out": w_mix_out[0],
        "ffn1_w_gu": ffn1_w_gu[0], "ffn1_w_down": ffn1_w_down[0],
        "ffn2_w_gu": ffn2_w_gu[0], "ffn2_w_down": ffn2_w_down[0],
        "gate_bias": gate_bias[0], "conv_w": conv_w[0], "conv_b": conv_b[0], "dt_bias": dt_bias[0],
        "a_log": a_log[0], "ssm_d": ssm_d[0], "ssm_norm": ssm_norm[0],
        "q_norm": q_norm[0], "kv_norm": kv_norm[0]})

    xb = x_prompt.reshape(b * seq, D_MODEL)
    xs = jnp.concatenate([x_sample.reshape(nd, D_MODEL), meta_tokens.astype(F32)], axis=0)

    hb, ub = _ffn(xb, ffn1_norm[0], wts["ffn1_w_gu"], wts["ffn1_w_down"], mix_norm[0], emit_h=True, out_dtype=BF16)
    hs, us = _ffn(xs, ffn1_norm[0], wts["ffn1_w_gu"], wts["ffn1_w_down"], mix_norm[0], emit_h=True, out_dtype=BF16)
    zb, xbcb, gateb, miscb = _in_proj(ub, wts)
    zs, xbcs, gates, miscs = _in_proj(us, wts)

    L = SSD_L
    meta3 = lambda a: _pad_rows(a[nd:][None], 1, L)
    _, ssm_meta = _ssd(meta3(xbcs), meta3(miscs), meta3(zs),
                       jnp.zeros((1, CONV_WIDTH - 1, CONV_DIM), F32),
                       jnp.zeros((1, SSM_HEADS, SSM_HEAD_DIM, SSM_STATE), F32), wts, t_valid=n_meta)
    conv_meta = xbcs[nd + n_meta - (CONV_WIDTH - 1):nd + n_meta][None]
    yb, ssm_p = _ssd(xbcb.reshape(b, seq, -1), miscb.reshape(b, seq, -1), zb.reshape(b, seq, -1),
                     jnp.broadcast_to(conv_meta, (b,) + conv_meta.shape[1:]),
                     jnp.broadcast_to(ssm_meta, (b,) + ssm_meta.shape[1:]), wts, t_valid=None)
    dec3 = lambda a: _pad_rows(a[:nd].reshape(db, dseq, -1), 1, L)
    yd, ssm_s = _ssd(dec3(xbcs), dec3(miscs), dec3(zs), state_conv[0], state_ssm[0], wts, t_valid=dseq)
    yd = yd[:, :dseq].reshape(nd, D_INNER)
    conv_p = xbcb.reshape(b, seq, -1)[:, seq - (CONV_WIDTH - 1):]
    xbcd = xbcs[:nd].reshape(db, dseq, -1)
    conv_s = jnp.concatenate([state_conv[0].astype(F32), xbcd], axis=1)[:, -(CONV_WIDTH - 1):]

    tq = _row_tile(seq, 512)
    cos_b, sin_b = _rope_tables(jnp.arange(seq, dtype=jnp.int32))
    pos_s = jnp.concatenate([jnp.tile(past + jnp.arange(dseq, dtype=jnp.int32), db),
                             jnp.arange(n_meta, dtype=jnp.int32) - n_meta])
    cos_s, sin_s = _rope_tables(pos_s)
    qb, ckvb, krb = _mla_prep(miscb, cos_b, sin_b, seq // tq, wts, tm=tq)
    qs, ckvs, krs = _mla_prep(miscs, cos_s, sin_s, 1, wts, tm=miscs.shape[0])
    kb, vb = _kv_expand(ckvb, krb, wts, tm=tq)
    kp, vp = _kv_expand(_pad_rows(ckvs[nd:], 0, LANES), _pad_rows(krs[nd:], 0, LANES), wts, tm=LANES)
    tkd = -(-(past + dseq) // LANES) * LANES
    ckv_d = _pad_rows(jnp.concatenate([cache_ckv[0].astype(F32), ckvs[:nd].reshape(db, dseq, -1)], axis=1), 1, tkd)
    kr_cache = jnp.pad(cache_krope[0].astype(F32),
                       ((0, 0), (0, 0), (QK_NOPE_DIM, LANES - QK_NOPE_DIM - QK_ROPE_DIM)))
    kr_d = _pad_rows(jnp.concatenate([kr_cache, krs[:nd].reshape(db, dseq, -1)], axis=1), 1, tkd)
    kd, vd = _kv_expand(ckv_d.reshape(db * tkd, -1), kr_d.reshape(db * tkd, -1), wts,
                        tm=_row_tile(db * tkd, 512))

    hsplit = lambda a, nb: a.reshape(MLA_HEADS, nb, -1, LANES)
    ob = _attention(hsplit(qb, b), hsplit(kb, b), hsplit(vb, b), kp, vp,
                    causal=True, tq=_row_tile(seq, ATTN_TQ), tk=tq, prefix_valid=n_meta)
    od = _attention(hsplit(qs[:, :nd], db), hsplit(kd, db), hsplit(vd, db), kp, vp,
                    causal=False, tq=dseq, tk=tkd, prefix_valid=n_meta, kv_valid=past + dseq)

    h2b = _mix(hb, yb.reshape(b * seq, -1), ob.reshape(HEAD_PAIRS, b * seq, LANES), gateb, wts)
    h2d = _mix(hs[:nd], yd, od.reshape(HEAD_PAIRS, nd, LANES), gates[:nd], wts)
    (y_p,) = _ffn(h2b, ffn2_norm[0], wts["ffn2_w_gu"], wts["ffn2_w_down"], final_norm, emit_h=False, out_dtype=F32)
    (y_s,) = _ffn(h2d, ffn2_norm[0], wts["ffn2_w_gu"], wts["ffn2_w_down"], final_norm, emit_h=False, out_dtype=F32)

    kr32 = lambda a: a[..., QK_NOPE_DIM:QK_NOPE_DIM + QK_ROPE_DIM]
    ckv_meta = jnp.broadcast_to(ckvs[nd:][None], (b, n_meta, KV_LORA))
    kr_meta = jnp.broadcast_to(kr32(krs[nd:])[None], (b, n_meta, QK_ROPE_DIM))
    ckv_p = jnp.concatenate([ckv_meta, ckvb.reshape(b, seq, -1)], axis=1)
    kr_p = jnp.concatenate([kr_meta, kr32(krb).reshape(b, seq, -1)], axis=1)
    return (y_p.reshape(b, seq, D_MODEL), y_s.reshape(db, dseq, D_MODEL),
            ckv_p[None], kr_p[None], ssm_p[None], conv_p[None],
            ckvs[:nd].reshape(db, dseq, -1)[None], kr32(krs[:nd]).reshape(db, dseq, -1)[None],
            ssm_s[None], conv_s[None])
```

```python
import functools
import math

import jax
import jax.numpy as jnp
from jax import lax
from jax.experimental import pallas as pl
from jax.experimental.pallas import tpu as pltpu

F32 = jnp.float32
BF16 = jnp.bfloat16

D_MODEL = 1024
CHUNK = 64
FFN_DIM = 2816
NORM_EPS = 1e-6
D_INNER = 2048
SSM_HEAD_DIM = 64
SSM_HEADS = 32
SSM_GROUPS = 4
SSM_STATE = 128
CONV_WIDTH = 4
CONV_DIM = D_INNER + 2 * SSM_GROUPS * SSM_STATE
MLA_HEADS = 16
HEAD_PACK = 4
HEAD_SLABS = MLA_HEADS // HEAD_PACK
Q_LORA = 384
KV_LORA = 256
QK_NOPE_DIM = 64
QK_ROPE_DIM = 32
V_HEAD_DIM = 64
SLAB_W = HEAD_PACK * V_HEAD_DIM
ROPE_THETA = 10000.0
OFF_Z = 0
OFF_XBC = OFF_Z + D_INNER
OFF_DT = OFF_XBC + CONV_DIM
OFF_CQ = OFF_DT + SSM_HEADS
OFF_CKV = OFF_CQ + Q_LORA
OFF_KROPE = OFF_CKV + KV_LORA
OFF_GATE = OFF_KROPE + QK_ROPE_DIM

LANES = 128
SSD_L = 128
FFN_TF = 256
HEAD_UNROLL = 8
ATTN_TQ = 1024
NEG_BIG = -1e30
VMEM_LIMIT = 52 * 1024 * 1024

MISC_CQ = 0
MISC_CKV = 384
MISC_KRA = 640
MISC_KRB = 768
MISC_DT = 896
MISC_W = 1024


def _cparams(n_grid):
    return pltpu.CompilerParams(dimension_semantics=("arbitrary",) * n_grid,
                                vmem_limit_bytes=VMEM_LIMIT)


def _row_tile(m, pref):
    if m <= pref:
        return m
    t = pref
    while t >= LANES:
        if m % t == 0:
            return t
        t //= 2
    return m


def _rms(x, g):
    return x * lax.rsqrt(jnp.mean(x * x, axis=-1, keepdims=True) + NORM_EPS) * g


def _dot(a, b):
    return jnp.dot(a, b, preferred_element_type=F32)


def _ffn_body(x_ref, g1_ref, wgu_ref, wd_ref, g2_ref, *outs, emit_h):
    x = x_ref[...]
    xn = _rms(x, g1_ref[...]).astype(BF16)
    acc = None
    for f in range(0, FFN_DIM, FFN_TF):
        g = _dot(xn, wgu_ref[:, f:f + FFN_TF])
        u = _dot(xn, wgu_ref[:, FFN_DIM + f:FFN_DIM + f + FFN_TF])
        act = (g * jax.nn.sigmoid(g) * u).astype(BF16)
        d = _dot(act, wd_ref[f:f + FFN_TF, :])
        acc = d if acc is None else acc + d
    h = x + 0.5 * acc
    if emit_h:
        outs[0][...] = h
    outs[-1][...] = _rms(h, g2_ref[...]).astype(outs[-1].dtype)


def _ffn(x, g1, w_gu, w_down, g2, *, emit_h, out_dtype, tm_pref=512):
    m = x.shape[0]
    tm = _row_tile(m, tm_pref)
    row = pl.BlockSpec((tm, D_MODEL), lambda i: (i, 0))
    vec = pl.BlockSpec((1, D_MODEL), lambda i: (0, 0))
    out_shape = [jax.ShapeDtypeStruct((m, D_MODEL), out_dtype)]
    out_specs = [row]
    if emit_h:
        out_shape = [jax.ShapeDtypeStruct((m, D_MODEL), F32)] + out_shape
        out_specs = [row, row]
    return pl.pallas_call(
        functools.partial(_ffn_body, emit_h=emit_h),
        out_shape=out_shape,
        grid=(m // tm,),
        in_specs=[row, vec,
                  pl.BlockSpec((D_MODEL, 2 * FFN_DIM), lambda i: (0, 0), pipeline_mode=pl.Buffered(1)),
                  pl.BlockSpec((FFN_DIM, D_MODEL), lambda i: (0, 0), pipeline_mode=pl.Buffered(1)),
                  vec],
        out_specs=out_specs,
        compiler_params=_cparams(1),
        name="ffn",
    )(x, g1.reshape(1, -1), w_gu, w_down, g2.reshape(1, -1))


def _softplus(x):
    return jnp.maximum(x, 0.0) + jnp.log1p(jnp.exp(-jnp.abs(x)))


def _expand_heads(x, e):
    hi = x.astype(BF16)
    lo = (x - hi.astype(F32)).astype(BF16)
    return _dot(hi, e) + _dot(lo, e)


def _ssd_body(xbc_ref, dtb_ref, zs_ref, convp_ref, ssmp_ref, cw_ref, cb_ref, dtbias_ref, alog_ref,
              dvec_ref, nrm_ref, e_ref, y_ref, ssm_ref, xpad, state, y_scr, *, t_valid):
    t = pl.program_id(1)
    L = SSD_L
    gs = SSM_STATE
    hg = SSM_HEADS // SSM_GROUPS
    gw = hg * SSM_HEAD_DIM

    @pl.when(t == 0)
    def _():
        xpad[0:8, :] = convp_ref[...]
        state[...] = ssmp_ref[...].T

    xpad[8:8 + L, :] = xbc_ref[...]

    def conv_act(c0, width):
        cs = slice(c0, c0 + width)
        acc = cb_ref[:, cs] + xpad[5:5 + L, cs] * cw_ref[0:1, cs]
        for k in range(1, CONV_WIDTH):
            acc = acc + xpad[5 + k:5 + k + L, cs] * cw_ref[k:k + 1, cs]
        return acc * jax.nn.sigmoid(acc)

    dt =_softplus(dtb_ref[:, 0:SSM_HEADS] + dtbias_ref[...])
    if t_valid is not None:
        rows = t * L + lax.broadcasted_iota(jnp.int32, (L, SSM_HEADS), 0)
        dt = jnp.where(rows < t_valid, dt, 0.0)
    da = dt * (-jnp.exp(alog_ref[...]))
    ri = lax.broadcasted_iota(jnp.int32, (L, L), 0)
    ci = lax.broadcasted_iota(jnp.int32, (L, L), 1)
    causal = ri >= ci
    a_cs = jnp.dot(causal.astype(F32), da, precision=lax.Precision.HIGHEST,
                   preferred_element_type=F32)
    eye = (lax.broadcasted_iota(jnp.int32, (SSM_HEADS, SSM_HEADS), 0)
           == lax.broadcasted_iota(jnp.int32, (SSM_HEADS, SSM_HEADS), 1)).astype(F32)
    a_cs_t = lax.dot_general(eye, a_cs, (((1,), (1,)), ((), ())), precision=lax.Precision.HIGHEST,
                             preferred_element_type=F32)
    total = a_cs[L - 1:L, :]

    exp_cs = jnp.exp(a_cs)
    w_in = jnp.exp(total - a_cs) * dt
    exp_tot = jnp.broadcast_to(jnp.exp(total), (8, SSM_HEADS))

    for g in range(SSM_GROUPS):
        gsl = slice(g * gw, (g + 1) * gw)
        xs = conv_act(g * gw, gw)
        bg = conv_act(D_INNER + g * gs, gs).astype(BF16)
        cg = conv_act(D_INNER + SSM_GROUPS * gs + g * gs, gs).astype(BF16)
        e = e_ref[:, gsl]
        xdt = xs * _expand_heads(dt, e)
        wx = (xs * _expand_heads(w_in, e)).astype(BF16)
        cb = lax.dot_general(cg, bg, (((1,), (1,)), ((), ())), preferred_element_type=F32)
        cb = jnp.where(causal, cb, 0.0)
        for hh in range(hg):
            h = g * hg + hh
            seg = a_cs[:, h:h + 1] - a_cs_t[h:h + 1, :]
            mh = (cb * jnp.exp(jnp.minimum(seg, 0.0))).astype(BF16)
            cs = slice(hh * SSM_HEAD_DIM, (hh + 1) * SSM_HEAD_DIM)
            y_scr[:, cs] = _dot(mh, xdt[:, cs].astype(BF16))
        st = state[:, gsl]
        y = y_scr[...] + _dot(cg, st.astype(BF16)) * _expand_heads(exp_cs, e) + dvec_ref[:, gsl] * xs
        upd = lax.dot_general(bg, wx, (((0,), (0,)), ((), ())), preferred_element_type=F32)
        state[:, gsl] = st * _expand_heads(exp_tot, e)[0:1, :] + upd
        y = y * zs_ref[:, gsl].astype(F32)
        y = y * lax.rsqrt(jnp.mean(y * y, axis=-1, keepdims=True) + NORM_EPS)
        y_ref[:, gsl] = (y * nrm_ref[:, gsl]).astype(y_ref.dtype)

    xpad[0:8, :] = xpad[L:L + 8, :]

    @pl.when(t == pl.num_programs(1) - 1)
    def _():
        ssm_ref[...] = state[...].T


def _ssd(xbc, misc, z, conv_prev, ssm_prev, wts, *, t_valid):
    b, t, _ = xbc.shape
    assert t % SSD_L == 0
    convp = jnp.pad(conv_prev.astype(F32), ((0, 0), (8 - (CONV_WIDTH - 1), 0), (0, 0)))
    ssmp = ssm_prev.astype(F32).reshape(b, D_INNER, SSM_STATE)
    tok = lambda w: pl.BlockSpec((None, SSD_L, w), lambda bi, ti: (bi, ti, 0))
    full = lambda r, c: pl.BlockSpec((r, c), lambda bi, ti: (0, 0))
    y, ssm = pl.pallas_call(
        functools.partial(_ssd_body, t_valid=t_valid),
        out_shape=[jax.ShapeDtypeStruct((b, t, D_INNER), BF16),
                   jax.ShapeDtypeStruct((b, D_INNER, SSM_STATE), F32)],
        grid=(b, t // SSD_L),
        in_specs=[tok(CONV_DIM),
                  pl.BlockSpec((None, SSD_L, LANES), lambda bi, ti: (bi, ti, MISC_DT // LANES)),
                  tok(D_INNER),
                  pl.BlockSpec((None, 8, CONV_DIM), lambda bi, ti: (bi, 0, 0)),
                  pl.BlockSpec((None, D_INNER, SSM_STATE), lambda bi, ti: (bi, 0, 0)),
                  full(CONV_WIDTH, CONV_DIM), full(1, CONV_DIM), full(1, SSM_HEADS), full(1, SSM_HEADS),
                  full(1, D_INNER), full(1, D_INNER), full(SSM_HEADS, D_INNER)],
        out_specs=[tok(D_INNER),
                   pl.BlockSpec((None, D_INNER, SSM_STATE), lambda bi, ti: (bi, 0, 0))],
        scratch_shapes=[pltpu.VMEM((SSD_L + 8, CONV_DIM), F32),
                        pltpu.VMEM((SSM_STATE, D_INNER), F32),
                        pltpu.VMEM((SSD_L, D_INNER // SSM_GROUPS), F32)],
        compiler_params=_cparams(2),
        name="ssd",
    )(xbc, misc, z, convp, ssmp, wts["conv_w"], wts["conv_b"], wts["dt_bias"], wts["a_log"],
      wts["dvec"], wts["ssm_norm"], wts["expand"])
    return y, ssm.reshape(b, SSM_HEADS, SSM_HEAD_DIM, SSM_STATE)


def _mla_prep_body(misc_ref, cos_ref, sin_ref, qn_ref, kvn_ref, w1_ref, w2_ref, q_ref, ckv_ref, kr_ref):
    cos = cos_ref[...]
    sin = sin_ref[...]
    qn = _rms(misc_ref[:, MISC_CQ:MISC_CQ + Q_LORA], qn_ref[...]).astype(BF16)
    q1 = _dot(qn, w1_ref[...])
    q2 = _dot(qn, w2_ref[...])
    scale = (QK_NOPE_DIM + QK_ROPE_DIM) ** -0.5
    for h in range(MLA_HEADS):
        hs = slice(h * LANES, (h + 1) * LANES)
        q_ref[h] = ((q1[:, hs] * cos + q2[:, hs] * sin) * scale).astype(BF16)
    ckv_ref[...] = _rms(misc_ref[:, MISC_CKV:MISC_CKV + KV_LORA], kvn_ref[...])
    kr_ref[...] = misc_ref[:, MISC_KRA:MISC_KRA + LANES] * cos + misc_ref[:, MISC_KRB:MISC_KRB + LANES] * sin


def _mla_prep(misc, cos, sin, n_pos_blocks, wts, *, tm):
    m = misc.shape[0]
    assert m % tm == 0
    return pl.pallas_call(
        _mla_prep_body,
        out_shape=[jax.ShapeDtypeStruct((MLA_HEADS, m, LANES), BF16),
                   jax.ShapeDtypeStruct((m, KV_LORA), F32),
                   jax.ShapeDtypeStruct((m, LANES), F32)],
        grid=(m // tm,),
        in_specs=[pl.BlockSpec((tm, MISC_W), lambda i: (i, 0)),
                  pl.BlockSpec((tm, LANES), lambda i: (i % n_pos_blocks, 0)),
                  pl.BlockSpec((tm, LANES), lambda i: (i % n_pos_blocks, 0)),
                  pl.BlockSpec((1, Q_LORA), lambda i: (0, 0)),
                  pl.BlockSpec((1, KV_LORA), lambda i: (0, 0)),
                  pl.BlockSpec((Q_LORA, MLA_HEADS * LANES), lambda i: (0, 0)),
                  pl.BlockSpec((Q_LORA, MLA_HEADS * LANES), lambda i: (0, 0))],
        out_specs=[pl.BlockSpec((MLA_HEADS, tm, LANES), lambda i: (0, i, 0)),
                   pl.BlockSpec((tm, KV_LORA), lambda i: (i, 0)),
                   pl.BlockSpec((tm, LANES), lambda i: (i, 0))],
        compiler_params=_cparams(1),
        name="mla_prep",
    )(misc, cos, sin, wts["q_norm"], wts["kv_norm"], wts["w_q1"], wts["w_q2"])


def _kv_body(ckv_ref, kr_ref, wk_ref, wv_ref, one_ref, k_ref, v_ref):
    c = ckv_ref[...].astype(BF16)
    k = _dot(c, wk_ref[...])
    v = _dot(c, wv_ref[...])
    kr = kr_ref[...]
    one = one_ref[...]
    for h in range(MLA_HEADS):
        hs = slice(h * LANES, (h + 1) * LANES)
        k_ref[h] = (k[:, hs] + kr).astype(BF16)
        v_ref[h] = (v[:, hs] + one).astype(BF16)


def _kv_expand(ckv, kr128, wts, *, tm):
    m = ckv.shape[0]
    assert m % tm == 0
    hw = MLA_HEADS * LANES
    return pl.pallas_call(
        _kv_body,
        out_shape=[jax.ShapeDtypeStruct((MLA_HEADS, m, LANES), BF16)] * 2,
        grid=(m // tm,),
        in_specs=[pl.BlockSpec((tm, KV_LORA), lambda i: (i, 0)),
                  pl.BlockSpec((tm, LANES), lambda i: (i, 0)),
                  pl.BlockSpec((KV_LORA, hw), lambda i: (0, 0)),
                  pl.BlockSpec((KV_LORA, hw), lambda i: (0, 0)),
                  pl.BlockSpec((1, LANES), lambda i: (0, 0))],
        out_specs=[pl.BlockSpec((MLA_HEADS, tm, LANES), lambda i: (0, i, 0))] * 2,
        compiler_params=_cparams(1),
        name="kv_expand",
    )(ckv, kr128, wts["w_k"], wts["w_v"], wts["one_col"])


def _attn_body(q_ref, k_ref, v_ref, kp_ref, vp_ref, o_ref, *scr, causal, prefix_valid, kv_valid):
    acc_scrs, m_scrs = scr[:HEAD_UNROLL], scr[HEAD_UNROLL:]
    i = pl.program_id(1)
    j = pl.program_id(2)
    tq = q_ref.shape[1]
    tk = k_ref.shape[1]
    tp = kp_ref.shape[1]
    n_trips = MLA_HEADS // HEAD_UNROLL

    def step(kr, vr, mask, row0=0):
        w = kr.shape[1]
        rows = pl.ds(row0, tq - row0)

        def body(hp, carry):
            for u in range(HEAD_UNROLL):
                h = hp * HEAD_UNROLL + u
                acc_scr, m_scr = acc_scrs[u], m_scrs[u]
                s = lax.dot_general(q_ref[h, rows, :], kr[h], (((1,), (1,)), ((), ())),
                                    preferred_element_type=F32)
                if mask is not None:
                    s = jnp.where(mask, s, NEG_BIG)
                m_prev = m_scr[hp, rows, :]
                m_new = jnp.maximum(m_prev, jnp.max(s, axis=1, keepdims=True))
                alpha = jnp.exp(m_prev - m_new)
                p = jnp.exp(s - pltpu.repeat(m_new, w // LANES, 1)).astype(BF16)
                acc_scr[hp, rows, :] = acc_scr[hp, rows, :] * alpha + _dot(p, vr[h])
                m_scr[hp, rows, :] = m_new
            return carry

        lax.fori_loop(0, n_trips, body, 0)

    def finalize():
        lane = lax.broadcasted_iota(jnp.int32, (tq, LANES), 1)

        def pair(u, hp):
            a = acc_scrs[u][hp]
            b = acc_scrs[u + 1][hp]
            oa = a / a[:, V_HEAD_DIM:V_HEAD_DIM + 1]
            ob = b / b[:, V_HEAD_DIM:V_HEAD_DIM + 1]
            return jnp.where(lane < V_HEAD_DIM, oa, pltpu.roll(ob, V_HEAD_DIM, 1))

        def body(hp, carry):
            for u in range(0, HEAD_UNROLL, HEAD_PACK):
                slab = jnp.concatenate([pair(u + 2 * k, hp) for k in range(HEAD_PACK // 2)], axis=1)
                o_ref[hp * (HEAD_UNROLL // HEAD_PACK) + u // HEAD_PACK] = slab.astype(o_ref.dtype)
            return carry

        lax.fori_loop(0, n_trips, body, 0)

    @pl.when(j == 0)
    def _():
        for u in range(HEAD_UNROLL):
            acc_scrs[u][...] = jnp.zeros_like(acc_scrs[u])
            m_scrs[u][...] = jnp.full_like(m_scrs[u], NEG_BIG)
        step(kp_ref, vp_ref, lax.broadcasted_iota(jnp.int32, (tq, tp), 1) < prefix_valid)

    if causal:
        r = tq // tk
        @pl.when(j < r * i)
        def _():
            step(k_ref, v_ref, None)

        for jj in range(r):
            @pl.when(j == r * i + jj)
            def _(jj=jj):
                row0 = jj * tk
                qc = (i * tq + row0 + lax.broadcasted_iota(jnp.int32, (tq - row0, tk), 0)) // CHUNK
                kc = (j * tk + lax.broadcasted_iota(jnp.int32, (tq - row0, tk), 1)) // CHUNK
                step(k_ref, v_ref, kc <= qc, row0)

        @pl.when(j == r * (i + 1) - 1)
        def _():
            finalize()
    else:
        mask = None
        if kv_valid is not None:
            mask = (j * tk + lax.broadcasted_iota(jnp.int32, (tq, tk), 1)) < kv_valid
        step(k_ref, v_ref, mask)

        @pl.when(j == pl.num_programs(2) - 1)
        def _():
            finalize()


def _attention(q, k, v, kp, vp, *, causal, tq, tk, prefix_valid, kv_valid=None):
    hh, b, t, _ = q.shape
    tkv = k.shape[2]
    assert t % tq == 0 and tkv % tk == 0
    if causal:
        assert tq % tk == 0 and t == tkv and tk % CHUNK == 0
        r = tq // tk
        kv_idx = lambda bi, i, j: (0, bi, jnp.minimum(r * (i + 1) - 1, j), 0)
    else:
        kv_idx = lambda bi, i, j: (0, bi, j, 0)
    tp = kp.shape[1]
    return pl.pallas_call(
        functools.partial(_attn_body, causal=causal, prefix_valid=prefix_valid, kv_valid=kv_valid),
        out_shape=jax.ShapeDtypeStruct((HEAD_SLABS, b, t, SLAB_W), BF16),
        grid=(b, t // tq, tkv // tk),
        in_specs=[pl.BlockSpec((hh, None, tq, LANES), lambda bi, i, j: (0, bi, i, 0)),
                  pl.BlockSpec((hh, None, tk, LANES), kv_idx),
                  pl.BlockSpec((hh, None, tk, LANES), kv_idx),
                  pl.BlockSpec((hh, tp, LANES), lambda bi, i, j: (0, 0, 0)),
                  pl.BlockSpec((hh, tp, LANES), lambda bi, i, j: (0, 0, 0))],
        out_specs=pl.BlockSpec((HEAD_SLABS, None, tq, SLAB_W), lambda bi, i, j: (0, bi, i, 0)),
        scratch_shapes=[pltpu.VMEM((hh // HEAD_UNROLL, tq, LANES), F32)] * (2 * HEAD_UNROLL),
        compiler_params=_cparams(3),
        name="attention",
    )(q, k, v, kp, vp)


def _mix_body(h_ref, y_ref, o_ref, gate_ref, ws_ref, wm_ref, wx_ref, out_ref):
    y_s = _dot(y_ref[...], ws_ref[...])
    y_m = _dot(o_ref[0], wm_ref[0])
    for hp in range(1, HEAD_SLABS):
        y_m = y_m + _dot(o_ref[hp], wm_ref[hp])
    merged = gate_ref[:, :D_MODEL].astype(F32) * y_s + gate_ref[:, D_MODEL:].astype(F32) * y_m
    out_ref[...] = h_ref[...] + _dot(merged.astype(BF16), wx_ref[...])


def _mix(h, y, o, gate, wts, *, tm_pref=256):
    m = h.shape[0]
    tm = _row_tile(m, tm_pref)
    return pl.pallas_call(
        _mix_body,
        out_shape=jax.ShapeDtypeStruct((m, D_MODEL), F32),
        grid=(m // tm,),
        in_specs=[pl.BlockSpec((tm, D_MODEL), lambda i: (i, 0)),
                  pl.BlockSpec((tm, D_INNER), lambda i: (i, 0)),
                  pl.BlockSpec((HEAD_SLABS, tm, SLAB_W), lambda i: (0, i, 0)),
                  pl.BlockSpec((tm, 2 * D_MODEL), lambda i: (i, 0)),
                  pl.BlockSpec((D_INNER, D_MODEL), lambda i: (0, 0)),
                  pl.BlockSpec((HEAD_SLABS, SLAB_W, D_MODEL), lambda i: (0, 0, 0)),
                  pl.BlockSpec((D_MODEL, D_MODEL), lambda i: (0, 0))],
        out_specs=pl.BlockSpec((tm, D_MODEL), lambda i: (i, 0)),
        compiler_params=_cparams(1),
        name="mix",
    )(h, y, o, gate, wts["w_ssm_out"], wts["w_mla_out"], wts["w_mix_out"])


def _prep_weights(p):
    w_in = p["w_in"]
    w_kr = w_in[:, OFF_KROPE:OFF_GATE]
    half = QK_ROPE_DIM // 2
    w_kr_sw = jnp.concatenate([w_kr[:, half:], w_kr[:, :half]], axis=1)
    zc = lambda n: jnp.zeros((D_MODEL, n), F32)
    w_misc = jnp.concatenate([
        w_in[:, OFF_CQ:OFF_CKV], w_in[:, OFF_CKV:OFF_KROPE],
        zc(QK_NOPE_DIM), w_kr, zc(LANES - QK_NOPE_DIM - QK_ROPE_DIM),
        zc(QK_NOPE_DIM), w_kr_sw, zc(LANES - QK_NOPE_DIM - QK_ROPE_DIM),
        w_in[:, OFF_DT:OFF_CQ], zc(LANES - SSM_HEADS)], axis=1)

    wq = p["w_uq"].reshape(Q_LORA, MLA_HEADS, QK_NOPE_DIM + QK_ROPE_DIM)
    q_nope, q_rope = wq[..., :QK_NOPE_DIM], wq[..., QK_NOPE_DIM:]
    q_rope_sw = jnp.concatenate([q_rope[..., half:], q_rope[..., :half]], axis=-1)
    zq = lambda n: jnp.zeros((Q_LORA, MLA_HEADS, n), F32)
    pad = LANES - QK_NOPE_DIM - QK_ROPE_DIM
    w_q1 = jnp.concatenate([q_nope, q_rope, zq(pad)], axis=-1).reshape(Q_LORA, MLA_HEADS * LANES)
    w_q2 = jnp.concatenate([zq(QK_NOPE_DIM), q_rope_sw, zq(pad)], axis=-1).reshape(Q_LORA, MLA_HEADS * LANES)

    wkv = p["w_ukv"].reshape(KV_LORA, MLA_HEADS, QK_NOPE_DIM + V_HEAD_DIM)
    zk = jnp.zeros((KV_LORA, MLA_HEADS, LANES - QK_NOPE_DIM), F32)
    w_k = jnp.concatenate([wkv[..., :QK_NOPE_DIM], zk], axis=-1).reshape(KV_LORA, MLA_HEADS * LANES)
    zv = jnp.zeros((KV_LORA, MLA_HEADS, LANES - V_HEAD_DIM), F32)
    w_v = jnp.concatenate([wkv[..., QK_NOPE_DIM:], zv], axis=-1).reshape(KV_LORA, MLA_HEADS * LANES)

    w_mla_out = p["mla_w_out"].reshape(HEAD_SLABS, SLAB_W, D_MODEL)

    expand = jnp.repeat(jnp.eye(SSM_HEADS, dtype=F32), SSM_HEAD_DIM, axis=1)
    one_col = (jnp.arange(LANES) == V_HEAD_DIM).astype(F32).reshape(1, LANES)
    return {
        "w_z": w_in[:, OFF_Z:OFF_XBC].astype(BF16),
        "w_xbc": w_in[:, OFF_XBC:OFF_DT].astype(BF16),
        "w_gate": w_in[:, OFF_GATE:].astype(BF16),
        "w_misc": w_misc.astype(BF16),
        "w_q1": w_q1.astype(BF16), "w_q2": w_q2.astype(BF16),
        "w_k": w_k.astype(BF16), "w_v": w_v.astype(BF16),
        "w_mla_out": w_mla_out.astype(BF16),
        "w_ssm_out": p["ssm_w_out"].astype(BF16),
        "w_mix_out": p["w_mix_out"].astype(BF16),
        "ffn1_w_gu": p["ffn1_w_gu"].astype(BF16), "ffn1_w_down": p["ffn1_w_down"].astype(BF16),
        "ffn2_w_gu": p["ffn2_w_gu"].astype(BF16), "ffn2_w_down": p["ffn2_w_down"].astype(BF16),
        "gate_bias": p["gate_bias"].reshape(1, -1),
        "conv_w": p["conv_w"], "conv_b": p["conv_b"].reshape(1, -1),
        "dt_bias": p["dt_bias"].reshape(1, -1), "a_log": p["a_log"].reshape(1, -1),
        "dvec": jnp.repeat(p["ssm_d"], SSM_HEAD_DIM).reshape(1, -1),
        "ssm_norm": p["ssm_norm"].reshape(1, -1),
        "expand": expand.astype(BF16), "one_col": one_col,
        "q_norm": p["q_norm"].reshape(1, -1), "kv_norm": p["kv_norm"].reshape(1, -1),
    }


def _rope_tables(pos):
    half = QK_ROPE_DIM // 2
    inv = ROPE_THETA ** (-jnp.arange(half, dtype=F32) / half)
    ang = pos.astype(F32)[:, None] * inv[None, :]
    c, s = jnp.cos(ang), jnp.sin(ang)
    t = pos.shape[0]
    pad = jnp.zeros((t, LANES - QK_NOPE_DIM - QK_ROPE_DIM), F32)
    cos = jnp.concatenate([jnp.ones((t, QK_NOPE_DIM), F32), c, c, pad], axis=1)
    sin = jnp.concatenate([jnp.zeros((t, QK_NOPE_DIM), F32), -s, s, pad], axis=1)
    return cos, sin


def _pad_rows(a, axis, to):
    padw = [(0, 0)] * a.ndim
    padw[axis] = (0, to - a.shape[axis])
    return jnp.pad(a, padw)


PROJ_CW = 512


def _in_proj_body(u_ref, wz_ref, wx_ref, wg_ref, wm_ref, gb_ref, zs_ref, xbc_ref, gate_ref, misc_ref):
    x = u_ref[...]
    for c in range(0, D_INNER, PROJ_CW):
        cs = slice(c, c + PROJ_CW)
        z = _dot(x, wz_ref[:, cs])
        zs_ref[:, cs] = (z * jax.nn.sigmoid(z)).astype(zs_ref.dtype)
    for c in range(0, CONV_DIM, PROJ_CW):
        cs = slice(c, c + PROJ_CW)
        xbc_ref[:, cs] = _dot(x, wx_ref[:, cs])
    for c in range(0, 2 * D_MODEL, PROJ_CW):
        cs = slice(c, c + PROJ_CW)
        gate_ref[:, cs] = jax.nn.sigmoid(_dot(x, wg_ref[:, cs]) + gb_ref[:, cs]).astype(gate_ref.dtype)
    for c in range(0, MISC_W, PROJ_CW):
        cs = slice(c, c + PROJ_CW)
        misc_ref[:, cs] = _dot(x, wm_ref[:, cs])


def _in_proj(u, wts, *, tm_pref=512):
    m = u.shape[0]
    tm = _row_tile(m, tm_pref)
    widths = (D_INNER, CONV_DIM, 2 * D_MODEL, MISC_W)
    dtypes = (BF16, F32, BF16, F32)
    resident = lambda n: pl.BlockSpec((D_MODEL, n), lambda i: (0, 0), pipeline_mode=pl.Buffered(1))
    return pl.pallas_call(
        _in_proj_body,
        out_shape=[jax.ShapeDtypeStruct((m, n), dt) for n, dt in zip(widths, dtypes)],
        grid=(m // tm,),
        in_specs=[pl.BlockSpec((tm, D_MODEL), lambda i: (i, 0))] + [resident(n) for n in widths]
                 + [pl.BlockSpec((1, 2 * D_MODEL), lambda i: (0, 0))],
        out_specs=[pl.BlockSpec((tm, n), lambda i: (i, 0)) for n in widths],
        compiler_params=_cparams(1),
        name="in_proj",
    )(u, wts["w_z"], wts["w_xbc"], wts["w_gate"], wts["w_misc"], wts["gate_bias"])


def kernel(x_prompt, x_sample, cache_ckv, cache_krope, state_ssm, state_conv, meta_tokens, ffn1_norm, ffn1_w_gu, ffn1_w_down, mix_norm, w_in, gate_bias, conv_w, conv_b, dt_bias, a_log, ssm_d, ssm_norm, ssm_w_out, q_norm, w_uq, kv_norm, w_ukv, mla_w_out, w_mix_out, ffn2_norm, ffn2_w_gu, ffn2_w_down, final_norm):
    b, seq, _ = x_prompt.shape
    db, dseq, _ = x_sample.shape
    past = cache_ckv.shape[2]
    n_meta = meta_tokens.shape[0]
    nd = db * dseq
    wts = _prep_weights({
        "w_in": w_in[0], "w_uq": w_uq[0], "w_ukv": w_ukv[0], "mla_w_out": mla_w_out[0],
        "ssm_w_out": ssm_w_out[0], "w_mix_out": w_mix_out[0],
        "ffn1_w_gu": ffn1_w_gu[0], "ffn1_w_down": ffn1_w_down[0],
        "ffn2_w_gu": ffn2_w_gu[0], "ffn2_w_down": ffn2_w_down[0],
        "gate_bias": gate_bias[0], "conv_w": conv_w[0], "conv_b": conv_b[0], "dt_bias": dt_bias[0],
        "a_log": a_log[0], "ssm_d": ssm_d[0], "ssm_norm": ssm_norm[0],
        "q_norm": q_norm[0], "kv_norm": kv_norm[0]})

    xb = x_prompt.reshape(b * seq, D_MODEL)
    xs = jnp.concatenate([x_sample.reshape(nd, D_MODEL), meta_tokens.astype(F32)], axis=0)

    hb, ub = _ffn(xb, ffn1_norm[0], wts["ffn1_w_gu"], wts["ffn1_w_down"], mix_norm[0], emit_h=True, out_dtype=BF16)
    hs, us = _ffn(xs, ffn1_norm[0], wts["ffn1_w_gu"], wts["ffn1_w_down"], mix_norm[0], emit_h=True, out_dtype=BF16)
    zb, xbcb, gateb, miscb = _in_proj(ub, wts)
    zs, xbcs, gates, miscs = _in_proj(us, wts)

    L = SSD_L
    meta3 = lambda a: _pad_rows(a[nd:][None], 1, L)
    _, ssm_meta = _ssd(meta3(xbcs), meta3(miscs), meta3(zs),
                       jnp.zeros((1, CONV_WIDTH - 1, CONV_DIM), F32),
                       jnp.zeros((1, SSM_HEADS, SSM_HEAD_DIM, SSM_STATE), F32), wts, t_valid=n_meta)
    conv_meta = xbcs[nd + n_meta - (CONV_WIDTH - 1):nd + n_meta][None]
    yb, ssm_p = _ssd(xbcb.reshape(b, seq, -1), miscb.reshape(b, seq, -1), zb.reshape(b, seq, -1),
                     jnp.broadcast_to(conv_meta, (b,) + conv_meta.shape[1:]),
                     jnp.broadcast_to(ssm_meta, (b,) + ssm_meta.shape[1:]), wts, t_valid=None)
    dec3 = lambda a: _pad_rows(a[:nd].reshape(db, dseq, -1), 1, L)
    yd, ssm_s = _ssd(dec3(xbcs), dec3(miscs), dec3(zs), state_conv[0], state_ssm[0], wts, t_valid=dseq)
    yd = yd[:, :dseq].reshape(nd, D_INNER)
    conv_p = xbcb.reshape(b, seq, -1)[:, seq - (CONV_WIDTH - 1):]
    xbcd = xbcs[:nd].reshape(db, dseq, -1)
    conv_s = jnp.concatenate([state_conv[0].astype(F32), xbcd], axis=1)[:, -(CONV_WIDTH - 1):]

    tq = _row_tile(seq, 512)
    cos_b, sin_b = _rope_tables(jnp.arange(seq, dtype=jnp.int32))
    pos_s = jnp.concatenate([jnp.tile(past + jnp.arange(dseq, dtype=jnp.int32), db),
                             jnp.arange(n_meta, dtype=jnp.int32) - n_meta])
    cos_s, sin_s = _rope_tables(pos_s)
    qb, ckvb, krb = _mla_prep(miscb, cos_b, sin_b, seq // tq, wts, tm=tq)
    qs, ckvs, krs = _mla_prep(miscs, cos_s, sin_s, 1, wts, tm=miscs.shape[0])
    kb, vb = _kv_expand(ckvb, krb, wts, tm=tq)
    kp, vp = _kv_expand(_pad_rows(ckvs[nd:], 0, LANES), _pad_rows(krs[nd:], 0, LANES), wts, tm=LANES)
    tkd = -(-(past + dseq) // LANES) * LANES
    ckv_d = _pad_rows(jnp.concatenate([cache_ckv[0].astype(F32), ckvs[:nd].reshape(db, dseq, -1)], axis=1), 1, tkd)
    kr_cache = jnp.pad(cache_krope[0].astype(F32),
                       ((0, 0), (0, 0), (QK_NOPE_DIM, LANES - QK_NOPE_DIM - QK_ROPE_DIM)))
    kr_d = _pad_rows(jnp.concatenate([kr_cache, krs[:nd].reshape(db, dseq, -1)], axis=1), 1, tkd)
    kd, vd = _kv_expand(ckv_d.reshape(db * tkd, -1), kr_d.reshape(db * tkd, -1), wts,
                        tm=_row_tile(db * tkd, 512))

    hsplit = lambda a, nb: a.reshape(MLA_HEADS, nb, -1, LANES)
    ob = _attention(hsplit(qb, b), hsplit(kb, b), hsplit(vb, b), kp, vp,
                    causal=True, tq=_row_tile(seq, ATTN_TQ), tk=tq, prefix_valid=n_meta)
    od = _attention(hsplit(qs[:, :nd], db), hsplit(kd, db), hsplit(vd, db), kp, vp,
                    causal=False, tq=dseq, tk=tkd, prefix_valid=n_meta, kv_valid=past + dseq)

    h2b = _mix(hb, yb.reshape(b * seq, -1), ob.reshape(HEAD_SLABS, b * seq, SLAB_W), gateb, wts)
    h2d = _mix(hs[:nd], yd, od.reshape(HEAD_SLABS, nd, SLAB_W), gates[:nd], wts)
    (y_p,) = _ffn(h2b, ffn2_norm[0], wts["ffn2_w_gu"], wts["ffn2_w_down"], final_norm, emit_h=False, out_dtype=F32)
    (y_s,) = _ffn(h2d, ffn2_norm[0], wts["ffn2_w_gu"], wts["ffn2_w_down"], final_norm, emit_h=False, out_dtype=F32)

    kr32 = lambda a: a[..., QK_NOPE_DIM:QK_NOPE_DIM + QK_ROPE_DIM]
    ckv_meta = jnp.broadcast_to(ckvs[nd:][None], (b, n_meta, KV_LORA))
    kr_meta = jnp.broadcast_to(kr32(krs[nd:])[None], (b, n_meta, QK_ROPE_DIM))
    ckv_p = jnp.concatenate([ckv_meta, ckvb.reshape(b, seq, -1)], axis=1)
    kr_p = jnp.concatenate([kr_meta, kr32(krb).reshape(b, seq, -1)], axis=1)
    return (y_p.reshape(b, seq, D_MODEL), y_s.reshape(db, dseq, D_MODEL),
            ckv_p[None], kr_p[None], ssm_p[None], conv_p[None],
            ckvs[:nd].reshape(db, dseq, -1)[None], kr32(krs[:nd]).reshape(db, dseq, -1)[None],
            ssm_s[None], conv_s[None])
```

```python
import functools
import math

import jax
import jax.numpy as jnp
import numpy as np
from jax import lax
from jax.experimental import pallas as pl
from jax.experimental.pallas import tpu as pltpu

F32 = jnp.float32
BF16 = jnp.bfloat16

D_MODEL = 1024
CHUNK = 64
FFN_DIM = 2816
NORM_EPS = 1e-6
D_INNER = 2048
SSM_HEAD_DIM = 64
SSM_HEADS = 32
SSM_GROUPS = 4
SSM_STATE = 128
CONV_WIDTH = 4
CONV_DIM = D_INNER + 2 * SSM_GROUPS * SSM_STATE
MLA_HEADS = 16
HEAD_PACK = 4
HEAD_SLABS = MLA_HEADS // HEAD_PACK
Q_LORA = 384
KV_LORA = 256
QK_NOPE_DIM = 64
QK_ROPE_DIM = 32
V_HEAD_DIM = 64
SLAB_W = HEAD_PACK * V_HEAD_DIM
ROPE_THETA = 10000.0
OFF_Z = 0
OFF_XBC = OFF_Z + D_INNER
OFF_DT = OFF_XBC + CONV_DIM
OFF_CQ = OFF_DT + SSM_HEADS
OFF_CKV = OFF_CQ + Q_LORA
OFF_KROPE = OFF_CKV + KV_LORA
OFF_GATE = OFF_KROPE + QK_ROPE_DIM

LANES = 128
SSD_L = 128
FFN_TF = 256
HEAD_UNROLL = 8
ATTN_TQ = 1024
NEG_BIG = -1e30
VMEM_LIMIT = 52 * 1024 * 1024

MISC_CQ = 0
MISC_CKV = 384
MISC_KRA = 640
MISC_KRB = 768
MISC_DT = 896
MISC_W = 1024


def _cparams(n_grid):
    return pltpu.CompilerParams(dimension_semantics=("arbitrary",) * n_grid,
                                vmem_limit_bytes=VMEM_LIMIT)


def _row_tile(m, pref):
    if m <= pref:
        return m
    t = pref
    while t >= LANES:
        if m % t == 0:
            return t
        t //= 2
    return m


def _rms(x, g):
    return x * lax.rsqrt(jnp.mean(x * x, axis=-1, keepdims=True) + NORM_EPS) * g


def _dot(a, b):
    return jnp.dot(a, b, preferred_element_type=F32)


def _ffn_body(x_ref, g1_ref, wgu_ref, wd_ref, g2_ref, *outs, emit_h):
    x = x_ref[...]
    xn = _rms(x, g1_ref[...]).astype(BF16)
    acc = None
    for f in range(0, FFN_DIM, FFN_TF):
        g = _dot(xn, wgu_ref[:, f:f + FFN_TF])
        u = _dot(xn, wgu_ref[:, FFN_DIM + f:FFN_DIM + f + FFN_TF])
        act = (g * jax.nn.sigmoid(g) * u).astype(BF16)
        d = _dot(act, wd_ref[f:f + FFN_TF, :])
        acc = d if acc is None else acc + d
    h = x + 0.5 * acc
    if emit_h:
        outs[0][...] = h
    outs[-1][...] = _rms(h, g2_ref[...]).astype(outs[-1].dtype)


def _ffn(x, g1, w_gu, w_down, g2, *, emit_h, out_dtype, tm_pref=512):
    m = x.shape[0]
    tm = _row_tile(m, tm_pref)
    row = pl.BlockSpec((tm, D_MODEL), lambda i: (i, 0))
    vec = pl.BlockSpec((1, D_MODEL), lambda i: (0, 0))
    out_shape = [jax.ShapeDtypeStruct((m, D_MODEL), out_dtype)]
    out_specs = [row]
    if emit_h:
        out_shape = [jax.ShapeDtypeStruct((m, D_MODEL), F32)] + out_shape
        out_specs = [row, row]
    return pl.pallas_call(
        functools.partial(_ffn_body, emit_h=emit_h),
        out_shape=out_shape,
        grid=(m // tm,),
        in_specs=[row, vec,
                  pl.BlockSpec((D_MODEL, 2 * FFN_DIM), lambda i: (0, 0), pipeline_mode=pl.Buffered(1)),
                  pl.BlockSpec((FFN_DIM, D_MODEL), lambda i: (0, 0), pipeline_mode=pl.Buffered(1)),
                  vec],
        out_specs=out_specs,
        compiler_params=_cparams(1),
        name="ffn",
    )(x, g1.reshape(1, -1), w_gu, w_down, g2.reshape(1, -1))


def _softplus(x):
    return jnp.maximum(x, 0.0) + jnp.log1p(jnp.exp(-jnp.abs(x)))


def _expand_heads(x, e):
    hi = x.astype(BF16)
    lo = (x - hi.astype(F32)).astype(BF16)
    return _dot(hi, e) + _dot(lo, e)


def _ssd_body(xbc_ref, dtb_ref, zs_ref, convp_ref, ssmp_ref, cw_ref, cb_ref, dtbias_ref, alog_ref,
              dvec_ref, nrm_ref, e_ref, y_ref, ssm_ref, xpad, state, y_scr, *, t_valid):
    t = pl.program_id(1)
    L = SSD_L
    gs = SSM_STATE
    hg = SSM_HEADS // SSM_GROUPS
    gw = hg * SSM_HEAD_DIM

    @pl.when(t == 0)
    def _():
        xpad[0:8, :] = convp_ref[...]
        state[...] = ssmp_ref[...].T

    xpad[8:8 + L, :] = xbc_ref[...]

    def conv_act(c0, width):
        cs = slice(c0, c0 + width)
        acc = cb_ref[:, cs] + xpad[5:5 + L, cs] * cw_ref[0:1, cs]
        for k in range(1, CONV_WIDTH):
            acc = acc + xpad[5 + k:5 + k + L, cs] * cw_ref[k:k + 1, cs]
        return acc * jax.nn.sigmoid(acc)

    dt =_softplus(dtb_ref[:, 0:SSM_HEADS] + dtbias_ref[...])
    if t_valid is not None:
        rows = t * L + lax.broadcasted_iota(jnp.int32, (L, SSM_HEADS), 0)
        dt = jnp.where(rows < t_valid, dt, 0.0)
    da = dt * (-jnp.exp(alog_ref[...]))
    ri = lax.broadcasted_iota(jnp.int32, (L, L), 0)
    ci = lax.broadcasted_iota(jnp.int32, (L, L), 1)
    causal = ri >= ci
    a_cs = jnp.dot(causal.astype(F32), da, precision=lax.Precision.HIGHEST,
                   preferred_element_type=F32)
    eye = (lax.broadcasted_iota(jnp.int32, (SSM_HEADS, SSM_HEADS), 0)
           == lax.broadcasted_iota(jnp.int32, (SSM_HEADS, SSM_HEADS), 1)).astype(F32)
    a_cs_t = lax.dot_general(eye, a_cs, (((1,), (1,)), ((), ())), precision=lax.Precision.HIGHEST,
                             preferred_element_type=F32)
    total = a_cs[L - 1:L, :]

    exp_cs = jnp.exp(a_cs)
    w_in = jnp.exp(total - a_cs) * dt
    exp_tot = jnp.broadcast_to(jnp.exp(total), (8, SSM_HEADS))

    for g in range(SSM_GROUPS):
        gsl = slice(g * gw, (g + 1) * gw)
        xs = conv_act(g * gw, gw)
        bg = conv_act(D_INNER + g * gs, gs).astype(BF16)
        cg = conv_act(D_INNER + SSM_GROUPS * gs + g * gs, gs).astype(BF16)
        e = e_ref[:, gsl]
        xdt = xs * _expand_heads(dt, e)
        wx = (xs * _expand_heads(w_in, e)).astype(BF16)
        cb = lax.dot_general(cg, bg, (((1,), (1,)), ((), ())), preferred_element_type=F32)
        cb = jnp.where(causal, cb, 0.0)
        for hh in range(hg):
            h = g * hg + hh
            seg = a_cs[:, h:h + 1] - a_cs_t[h:h + 1, :]
            mh = (cb * jnp.exp(jnp.minimum(seg, 0.0))).astype(BF16)
            cs = slice(hh * SSM_HEAD_DIM, (hh + 1) * SSM_HEAD_DIM)
            y_scr[:, cs] = _dot(mh, xdt[:, cs].astype(BF16))
        st = state[:, gsl]
        y = y_scr[...] + _dot(cg, st.astype(BF16)) * _expand_heads(exp_cs, e) + dvec_ref[:, gsl] * xs
        upd = lax.dot_general(bg, wx, (((0,), (0,)), ((), ())), preferred_element_type=F32)
        state[:, gsl] = st * _expand_heads(exp_tot, e)[0:1, :] + upd
        y = y * zs_ref[:, gsl].astype(F32)
        y = y * lax.rsqrt(jnp.mean(y * y, axis=-1, keepdims=True) + NORM_EPS)
        y_ref[:, gsl] = (y * nrm_ref[:, gsl]).astype(y_ref.dtype)

    xpad[0:8, :] = xpad[L:L + 8, :]

    @pl.when(t == pl.num_programs(1) - 1)
    def _():
        ssm_ref[...] = state[...].T


def _ssd(xbc, misc, z, conv_prev, ssm_prev, wts, *, t_valid):
    b, t, _ = xbc.shape
    assert t % SSD_L == 0
    convp = jnp.pad(conv_prev.astype(F32), ((0, 0), (8 - (CONV_WIDTH - 1), 0), (0, 0)))
    ssmp = ssm_prev.astype(F32).reshape(b, D_INNER, SSM_STATE)
    tok = lambda w: pl.BlockSpec((None, SSD_L, w), lambda bi, ti: (bi, ti, 0))
    full = lambda r, c: pl.BlockSpec((r, c), lambda bi, ti: (0, 0))
    y, ssm = pl.pallas_call(
        functools.partial(_ssd_body, t_valid=t_valid),
        out_shape=[jax.ShapeDtypeStruct((b, t, D_INNER), BF16),
                   jax.ShapeDtypeStruct((b, D_INNER, SSM_STATE), F32)],
        grid=(b, t // SSD_L),
        in_specs=[tok(CONV_DIM),
                  pl.BlockSpec((None, SSD_L, LANES), lambda bi, ti: (bi, ti, MISC_DT // LANES)),
                  tok(D_INNER),
                  pl.BlockSpec((None, 8, CONV_DIM), lambda bi, ti: (bi, 0, 0)),
                  pl.BlockSpec((None, D_INNER, SSM_STATE), lambda bi, ti: (bi, 0, 0)),
                  full(CONV_WIDTH, CONV_DIM), full(1, CONV_DIM), full(1, SSM_HEADS), full(1, SSM_HEADS),
                  full(1, D_INNER), full(1, D_INNER), full(SSM_HEADS, D_INNER)],
        out_specs=[tok(D_INNER),
                   pl.BlockSpec((None, D_INNER, SSM_STATE), lambda bi, ti: (bi, 0, 0))],
        scratch_shapes=[pltpu.VMEM((SSD_L + 8, CONV_DIM), F32),
                        pltpu.VMEM((SSM_STATE, D_INNER), F32),
                        pltpu.VMEM((SSD_L, D_INNER // SSM_GROUPS), F32)],
        compiler_params=_cparams(2),
        name="ssd",
    )(xbc, misc, z, convp, ssmp, wts["conv_w"], wts["conv_b"], wts["dt_bias"], wts["a_log"],
      wts["dvec"], wts["ssm_norm"], wts["expand"])
    return y, ssm.reshape(b, SSM_HEADS, SSM_HEAD_DIM, SSM_STATE)


def _mla_prep_body(misc_ref, cos_ref, sin_ref, qn_ref, kvn_ref, w1_ref, w2_ref, q_ref, ckv_ref, kr_ref):
    cos = cos_ref[...]
    sin = sin_ref[...]
    qn = _rms(misc_ref[:, MISC_CQ:MISC_CQ + Q_LORA], qn_ref[...]).astype(BF16)
    q1 = _dot(qn, w1_ref[...])
    q2 = _dot(qn, w2_ref[...])
    scale = (QK_NOPE_DIM + QK_ROPE_DIM) ** -0.5 * math.log2(math.e)
    for h in range(MLA_HEADS):
        hs = slice(h * LANES, (h + 1) * LANES)
        q_ref[h] = ((q1[:, hs] * cos + q2[:, hs] * sin) * scale).astype(BF16)
    ckv_ref[...] = _rms(misc_ref[:, MISC_CKV:MISC_CKV + KV_LORA], kvn_ref[...])
    kr_ref[...] = misc_ref[:, MISC_KRA:MISC_KRA + LANES] * cos + misc_ref[:, MISC_KRB:MISC_KRB + LANES] * sin


def _mla_prep(misc, cos, sin, n_pos_blocks, wts, *, tm):
    m = misc.shape[0]
    assert m % tm == 0
    return pl.pallas_call(
        _mla_prep_body,
        out_shape=[jax.ShapeDtypeStruct((MLA_HEADS, m, LANES), BF16),
                   jax.ShapeDtypeStruct((m, KV_LORA), F32),
                   jax.ShapeDtypeStruct((m, LANES), F32)],
        grid=(m // tm,),
        in_specs=[pl.BlockSpec((tm, MISC_W), lambda i: (i, 0)),
                  pl.BlockSpec((tm, LANES), lambda i: (i % n_pos_blocks, 0)),
                  pl.BlockSpec((tm, LANES), lambda i: (i % n_pos_blocks, 0)),
                  pl.BlockSpec((1, Q_LORA), lambda i: (0, 0)),
                  pl.BlockSpec((1, KV_LORA), lambda i: (0, 0)),
                  pl.BlockSpec((Q_LORA, MLA_HEADS * LANES), lambda i: (0, 0)),
                  pl.BlockSpec((Q_LORA, MLA_HEADS * LANES), lambda i: (0, 0))],
        out_specs=[pl.BlockSpec((MLA_HEADS, tm, LANES), lambda i: (0, i, 0)),
                   pl.BlockSpec((tm, KV_LORA), lambda i: (i, 0)),
                   pl.BlockSpec((tm, LANES), lambda i: (i, 0))],
        compiler_params=_cparams(1),
        name="mla_prep",
    )(misc, cos, sin, wts["q_norm"], wts["kv_norm"], wts["w_q1"], wts["w_q2"])


def _kv_body(ckv_ref, kr_ref, wk_ref, wv_ref, one_ref, kt_ref, v_ref):
    c = ckv_ref[...].astype(BF16)
    k = _dot(c, wk_ref[...])
    v = _dot(c, wv_ref[...])
    kr = kr_ref[...]
    one = one_ref[...]
    for h in range(MLA_HEADS):
        hs = slice(h * LANES, (h + 1) * LANES)
        kt_ref[h] = (k[:, hs] + kr).T.astype(BF16)
        v_ref[h] = (v[:, hs] + one).astype(BF16)


def _kv_expand(ckv, kr128, wts, *, tm):
    m = ckv.shape[0]
    assert m % tm == 0
    hw = MLA_HEADS * LANES
    return pl.pallas_call(
        _kv_body,
        out_shape=[jax.ShapeDtypeStruct((MLA_HEADS, LANES, m), BF16),
                   jax.ShapeDtypeStruct((MLA_HEADS, m, LANES), BF16)],
        grid=(m // tm,),
        in_specs=[pl.BlockSpec((tm, KV_LORA), lambda i: (i, 0)),
                  pl.BlockSpec((tm, LANES), lambda i: (i, 0)),
                  pl.BlockSpec((KV_LORA, hw), lambda i: (0, 0)),
                  pl.BlockSpec((KV_LORA, hw), lambda i: (0, 0)),
                  pl.BlockSpec((1, LANES), lambda i: (0, 0))],
        out_specs=[pl.BlockSpec((MLA_HEADS, LANES, tm), lambda i: (0, 0, i)),
                   pl.BlockSpec((MLA_HEADS, tm, LANES), lambda i: (0, i, 0))],
        compiler_params=_cparams(1),
        name="kv_expand",
    )(ckv, kr128, wts["w_k"], wts["w_v"], wts["one_col"])


def _attn_body(q_ref, kt_ref, v_ref, kpt_ref, vp_ref, o_ref, *scr, causal, prefix_valid, kv_valid):
    acc_scrs, m_scrs = scr[:HEAD_UNROLL], scr[HEAD_UNROLL:]
    i = pl.program_id(1)
    j = pl.program_id(2)
    tq = q_ref.shape[1]
    tk = kt_ref.shape[2]
    tp = kpt_ref.shape[2]
    n_trips = MLA_HEADS // HEAD_UNROLL

    def step(ktr, vr, mask, row0=0):
        w = ktr.shape[2]
        rows = pl.ds(row0, tq - row0)

        def body(hp, carry):
            for u in range(HEAD_UNROLL):
                h = hp * HEAD_UNROLL + u
                acc_scr, m_scr = acc_scrs[u], m_scrs[u]
                s = _dot(q_ref[h, rows, :], ktr[h])
                if mask is not None:
                    s = jnp.where(mask, s, NEG_BIG)
                m_prev = m_scr[hp, rows, :]
                m_new = jnp.maximum(m_prev, jnp.max(s, axis=1, keepdims=True))
                alpha = jnp.exp2(m_prev - m_new)
                p = jnp.exp2(s - pltpu.repeat(m_new, w // LANES, 1)).astype(BF16)
                acc_scr[hp, rows, :] = acc_scr[hp, rows, :] * alpha + _dot(p, vr[h])
                m_scr[hp, rows, :] = m_new
            return carry

        lax.fori_loop(0, n_trips, body, 0)

    def finalize():
        lane = lax.broadcasted_iota(jnp.int32, (tq, LANES), 1)

        def pair(u, hp):
            a = acc_scrs[u][hp]
            b = acc_scrs[u + 1][hp]
            oa = a / a[:, V_HEAD_DIM:V_HEAD_DIM + 1]
            ob = b / b[:, V_HEAD_DIM:V_HEAD_DIM + 1]
            return jnp.where(lane < V_HEAD_DIM, oa, pltpu.roll(ob, V_HEAD_DIM, 1))

        def body(hp, carry):
            for u in range(0, HEAD_UNROLL, HEAD_PACK):
                slab = jnp.concatenate([pair(u + 2 * k, hp) for k in range(HEAD_PACK // 2)], axis=1)
                o_ref[hp * (HEAD_UNROLL // HEAD_PACK) + u // HEAD_PACK] = slab.astype(o_ref.dtype)
            return carry

        lax.fori_loop(0, n_trips, body, 0)

    @pl.when(j == 0)
    def _():
        for u in range(HEAD_UNROLL):
            acc_scrs[u][...] = jnp.zeros_like(acc_scrs[u])
            m_scrs[u][...] = jnp.full_like(m_scrs[u], NEG_BIG)
        step(kpt_ref, vp_ref, lax.broadcasted_iota(jnp.int32, (tq, tp), 1) < prefix_valid)

    if causal:
        r = tq // tk
        @pl.when(j < r * i)
        def _():
            step(kt_ref, v_ref, None)

        for jj in range(r):
            @pl.when(j == r * i + jj)
            def _(jj=jj):
                row0 = jj * tk
                qc = (i * tq + row0 + lax.broadcasted_iota(jnp.int32, (tq - row0, tk), 0)) // CHUNK
                kc = (j * tk + lax.broadcasted_iota(jnp.int32, (tq - row0, tk), 1)) // CHUNK
                step(kt_ref, v_ref, kc <= qc, row0)

        @pl.when(j == r * (i + 1) - 1)
        def _():
            finalize()
    else:
        mask = None
        if kv_valid is not None:
            mask = (j * tk + lax.broadcasted_iota(jnp.int32, (tq, tk), 1)) < kv_valid
        step(kt_ref, v_ref, mask)

        @pl.when(j == pl.num_programs(2) - 1)
        def _():
            finalize()


def _attention(q, kt, v, kpt, vp, *, causal, tq, tk, prefix_valid, kv_valid=None):
    hh, b, t, _ = q.shape
    tkv = v.shape[2]
    assert t % tq == 0 and tkv % tk == 0
    nkb = tkv // tk
    if causal:
        assert tq % tk == 0 and t == tkv and tk % CHUNK == 0
        r = tq // tk
        kblk = lambda i, j: jnp.minimum(r * (i + 1) - 1, j)
    else:
        kblk = lambda i, j: j
    tp = vp.shape[1]
    return pl.pallas_call(
        functools.partial(_attn_body, causal=causal, prefix_valid=prefix_valid, kv_valid=kv_valid),
        out_shape=jax.ShapeDtypeStruct((HEAD_SLABS, b, t, SLAB_W), BF16),
        grid=(b, t // tq, nkb),
        in_specs=[pl.BlockSpec((hh, None, tq, LANES), lambda bi, i, j: (0, bi, i, 0)),
                  pl.BlockSpec((hh, LANES, tk), lambda bi, i, j: (0, 0, bi * nkb + kblk(i, j))),
                  pl.BlockSpec((hh, None, tk, LANES), lambda bi, i, j: (0, bi, kblk(i, j), 0)),
                  pl.BlockSpec((hh, LANES, tp), lambda bi, i, j: (0, 0, 0)),
                  pl.BlockSpec((hh, tp, LANES), lambda bi, i, j: (0, 0, 0))],
        out_specs=pl.BlockSpec((HEAD_SLABS, None, tq, SLAB_W), lambda bi, i, j: (0, bi, i, 0)),
        scratch_shapes=[pltpu.VMEM((hh // HEAD_UNROLL, tq, LANES), F32)] * (2 * HEAD_UNROLL),
        compiler_params=_cparams(3),
        name="attention",
    )(q, kt, v, kpt, vp)


def _mix_body(h_ref, y_ref, o_ref, gate_ref, ws_ref, wm_ref, wx_ref, out_ref):
    y_s = _dot(y_ref[...], ws_ref[...])
    y_m = _dot(o_ref[0], wm_ref[0])
    for hp in range(1, HEAD_SLABS):
        y_m = y_m + _dot(o_ref[hp], wm_ref[hp])
    merged = gate_ref[:, :D_MODEL].astype(F32) * y_s + gate_ref[:, D_MODEL:].astype(F32) * y_m
    out_ref[...] = h_ref[...] + _dot(merged.astype(BF16), wx_ref[...])


def _mix(h, y, o, gate, wts, *, tm_pref=256):
    m = h.shape[0]
    tm = _row_tile(m, tm_pref)
    return pl.pallas_call(
        _mix_body,
        out_shape=jax.ShapeDtypeStruct((m, D_MODEL), F32),
        grid=(m // tm,),
        in_specs=[pl.BlockSpec((tm, D_MODEL), lambda i: (i, 0)),
                  pl.BlockSpec((tm, D_INNER), lambda i: (i, 0)),
                  pl.BlockSpec((HEAD_SLABS, tm, SLAB_W), lambda i: (0, i, 0)),
                  pl.BlockSpec((tm, 2 * D_MODEL), lambda i: (i, 0)),
                  pl.BlockSpec((D_INNER, D_MODEL), lambda i: (0, 0)),
                  pl.BlockSpec((HEAD_SLABS, SLAB_W, D_MODEL), lambda i: (0, 0, 0)),
                  pl.BlockSpec((D_MODEL, D_MODEL), lambda i: (0, 0))],
        out_specs=pl.BlockSpec((tm, D_MODEL), lambda i: (i, 0)),
        compiler_params=_cparams(1),
        name="mix",
    )(h, y, o, gate, wts["w_ssm_out"], wts["w_mla_out"], wts["w_mix_out"])


def _prep_weights(p):
    w_in = p["w_in"]
    w_kr = w_in[:, OFF_KROPE:OFF_GATE]
    half = QK_ROPE_DIM // 2
    w_kr_sw = jnp.concatenate([w_kr[:, half:], w_kr[:, :half]], axis=1)
    zc = lambda n: jnp.zeros((D_MODEL, n), F32)
    w_misc = jnp.concatenate([
        w_in[:, OFF_CQ:OFF_CKV], w_in[:, OFF_CKV:OFF_KROPE],
        zc(QK_NOPE_DIM), w_kr, zc(LANES - QK_NOPE_DIM - QK_ROPE_DIM),
        zc(QK_NOPE_DIM), w_kr_sw, zc(LANES - QK_NOPE_DIM - QK_ROPE_DIM),
        w_in[:, OFF_DT:OFF_CQ], zc(LANES - SSM_HEADS)], axis=1)

    wq = p["w_uq"].reshape(Q_LORA, MLA_HEADS, QK_NOPE_DIM + QK_ROPE_DIM)
    q_nope, q_rope = wq[..., :QK_NOPE_DIM], wq[..., QK_NOPE_DIM:]
    q_rope_sw = jnp.concatenate([q_rope[..., half:], q_rope[..., :half]], axis=-1)
    zq = lambda n: jnp.zeros((Q_LORA, MLA_HEADS, n), F32)
    pad = LANES - QK_NOPE_DIM - QK_ROPE_DIM
    w_q1 = jnp.concatenate([q_nope, q_rope, zq(pad)], axis=-1).reshape(Q_LORA, MLA_HEADS * LANES)
    w_q2 = jnp.concatenate([zq(QK_NOPE_DIM), q_rope_sw, zq(pad)], axis=-1).reshape(Q_LORA, MLA_HEADS * LANES)

    wkv = p["w_ukv"].reshape(KV_LORA, MLA_HEADS, QK_NOPE_DIM + V_HEAD_DIM)
    zk = jnp.zeros((KV_LORA, MLA_HEADS, LANES - QK_NOPE_DIM), F32)
    w_k = jnp.concatenate([wkv[..., :QK_NOPE_DIM], zk], axis=-1).reshape(KV_LORA, MLA_HEADS * LANES)
    zv = jnp.zeros((KV_LORA, MLA_HEADS, LANES - V_HEAD_DIM), F32)
    w_v = jnp.concatenate([wkv[..., QK_NOPE_DIM:], zv], axis=-1).reshape(KV_LORA, MLA_HEADS * LANES)

    w_mla_out = p["mla_w_out"].reshape(HEAD_SLABS, SLAB_W, D_MODEL)

    expand = jnp.repeat(jnp.eye(SSM_HEADS, dtype=F32), SSM_HEAD_DIM, axis=1)
    one_col = (jnp.arange(LANES) == V_HEAD_DIM).astype(F32).reshape(1, LANES)
    return {
        "w_z": w_in[:, OFF_Z:OFF_XBC].astype(BF16),
        "w_xbc": w_in[:, OFF_XBC:OFF_DT].astype(BF16),
        "w_gate": w_in[:, OFF_GATE:].astype(BF16),
        "w_misc": w_misc.astype(BF16),
        "w_q1": w_q1.astype(BF16), "w_q2": w_q2.astype(BF16),
        "w_k": w_k.astype(BF16), "w_v": w_v.astype(BF16),
        "w_mla_out": w_mla_out.astype(BF16),
        "w_ssm_out": p["ssm_w_out"].astype(BF16),
        "w_mix_out": p["w_mix_out"].astype(BF16),
        "ffn1_w_gu": p["ffn1_w_gu"].astype(BF16), "ffn1_w_down": p["ffn1_w_down"].astype(BF16),
        "ffn2_w_gu": p["ffn2_w_gu"].astype(BF16), "ffn2_w_down": p["ffn2_w_down"].astype(BF16),
        "gate_bias": p["gate_bias"].reshape(1, -1),
        "conv_w": p["conv_w"], "conv_b": p["conv_b"].reshape(1, -1),
        "dt_bias": p["dt_bias"].reshape(1, -1), "a_log": p["a_log"].reshape(1, -1),
        "dvec": jnp.repeat(p["ssm_d"], SSM_HEAD_DIM).reshape(1, -1),
        "ssm_norm": p["ssm_norm"].reshape(1, -1),
        "expand": expand.astype(BF16), "one_col": one_col,
        "q_norm": p["q_norm"].reshape(1, -1), "kv_norm": p["kv_norm"].reshape(1, -1),
    }


def _rope_tables(pos):
    half = QK_ROPE_DIM // 2
    inv = np.float32(ROPE_THETA) ** (-np.arange(half, dtype=np.float32) / np.float32(half))
    ang = pos.astype(np.float32)[:, None] * inv[None, :]
    c, s = jnp.asarray(np.cos(ang).astype(np.float32)), jnp.asarray(np.sin(ang).astype(np.float32))
    t = pos.shape[0]
    pad = jnp.zeros((t, LANES - QK_NOPE_DIM - QK_ROPE_DIM), F32)
    cos = jnp.concatenate([jnp.ones((t, QK_NOPE_DIM), F32), c, c, pad], axis=1)
    sin = jnp.concatenate([jnp.zeros((t, QK_NOPE_DIM), F32), -s, s, pad], axis=1)
    return cos, sin


def _pad_rows(a, axis, to):
    padw = [(0, 0)] * a.ndim
    padw[axis] = (0, to - a.shape[axis])
    return jnp.pad(a, padw)


PROJ_CW = 512


def _in_proj_body(u_ref, wz_ref, wx_ref, wg_ref, wm_ref, gb_ref, zs_ref, xbc_ref, gate_ref, misc_ref):
    x = u_ref[...]
    for c in range(0, D_INNER, PROJ_CW):
        cs = slice(c, c + PROJ_CW)
        z = _dot(x, wz_ref[:, cs])
        zs_ref[:, cs] = (z * jax.nn.sigmoid(z)).astype(zs_ref.dtype)
    for c in range(0, CONV_DIM, PROJ_CW):
        cs = slice(c, c + PROJ_CW)
        xbc_ref[:, cs] = _dot(x, wx_ref[:, cs])
    for c in range(0, 2 * D_MODEL, PROJ_CW):
        cs = slice(c, c + PROJ_CW)
        gate_ref[:, cs] = jax.nn.sigmoid(_dot(x, wg_ref[:, cs]) + gb_ref[:, cs]).astype(gate_ref.dtype)
    for c in range(0, MISC_W, PROJ_CW):
        cs = slice(c, c + PROJ_CW)
        misc_ref[:, cs] = _dot(x, wm_ref[:, cs])


def _in_proj(u, wts, *, tm_pref=512):
    m = u.shape[0]
    tm = _row_tile(m, tm_pref)
    widths = (D_INNER, CONV_DIM, 2 * D_MODEL, MISC_W)
    dtypes = (BF16, F32, BF16, F32)
    resident = lambda n: pl.BlockSpec((D_MODEL, n), lambda i: (0, 0), pipeline_mode=pl.Buffered(1))
    return pl.pallas_call(
        _in_proj_body,
        out_shape=[jax.ShapeDtypeStruct((m, n), dt) for n, dt in zip(widths, dtypes)],
        grid=(m // tm,),
        in_specs=[pl.BlockSpec((tm, D_MODEL), lambda i: (i, 0))] + [resident(n) for n in widths]
                 + [pl.BlockSpec((1, 2 * D_MODEL), lambda i: (0, 0))],
        out_specs=[pl.BlockSpec((tm, n), lambda i: (i, 0)) for n in widths],
        compiler_params=_cparams(1),
        name="in_proj",
    )(u, wts["w_z"], wts["w_xbc"], wts["w_gate"], wts["w_misc"], wts["gate_bias"])


def kernel(x_prompt, x_sample, cache_ckv, cache_krope, state_ssm, state_conv, meta_tokens, ffn1_norm, ffn1_w_gu, ffn1_w_down, mix_norm, w_in, gate_bias, conv_w, conv_b, dt_bias, a_log, ssm_d, ssm_norm, ssm_w_out, q_norm, w_uq, kv_norm, w_ukv, mla_w_out, w_mix_out, ffn2_norm, ffn2_w_gu, ffn2_w_down, final_norm):
    b, seq, _ = x_prompt.shape
    db, dseq, _ = x_sample.shape
    past = cache_ckv.shape[2]
    n_meta = meta_tokens.shape[0]
    nd = db * dseq
    wts = _prep_weights({
        "w_in": w_in[0], "w_uq": w_uq[0], "w_ukv": w_ukv[0], "mla_w_out": mla_w_out[0],
        "ssm_w_out": ssm_w_out[0], "w_mix_out": w_mix_out[0],
        "ffn1_w_gu": ffn1_w_gu[0], "ffn1_w_down": ffn1_w_down[0],
        "ffn2_w_gu": ffn2_w_gu[0], "ffn2_w_down": ffn2_w_down[0],
        "gate_bias": gate_bias[0], "conv_w": conv_w[0], "conv_b": conv_b[0], "dt_bias": dt_bias[0],
        "a_log": a_log[0], "ssm_d": ssm_d[0], "ssm_norm": ssm_norm[0],
        "q_norm": q_norm[0], "kv_norm": kv_norm[0]})

    xb = x_prompt.reshape(b * seq, D_MODEL)
    xs = jnp.concatenate([x_sample.reshape(nd, D_MODEL), meta_tokens.astype(F32)], axis=0)

    hb, ub = _ffn(xb, ffn1_norm[0], wts["ffn1_w_gu"], wts["ffn1_w_down"], mix_norm[0], emit_h=True, out_dtype=BF16)
    hs, us = _ffn(xs, ffn1_norm[0], wts["ffn1_w_gu"], wts["ffn1_w_down"], mix_norm[0], emit_h=True, out_dtype=BF16)
    zb, xbcb, gateb, miscb = _in_proj(ub, wts)
    zs, xbcs, gates, miscs = _in_proj(us, wts)

    L = SSD_L
    meta3 = lambda a: _pad_rows(a[nd:][None], 1, L)
    _, ssm_meta = _ssd(meta3(xbcs), meta3(miscs), meta3(zs),
                       jnp.zeros((1, CONV_WIDTH - 1, CONV_DIM), F32),
                       jnp.zeros((1, SSM_HEADS, SSM_HEAD_DIM, SSM_STATE), F32), wts, t_valid=n_meta)
    conv_meta = xbcs[nd + n_meta - (CONV_WIDTH - 1):nd + n_meta][None]
    yb, ssm_p = _ssd(xbcb.reshape(b, seq, -1), miscb.reshape(b, seq, -1), zb.reshape(b, seq, -1),
                     jnp.broadcast_to(conv_meta, (b,) + conv_meta.shape[1:]),
                     jnp.broadcast_to(ssm_meta, (b,) + ssm_meta.shape[1:]), wts, t_valid=None)
    dec3 = lambda a: _pad_rows(a[:nd].reshape(db, dseq, -1), 1, L)
    yd, ssm_s = _ssd(dec3(xbcs), dec3(miscs), dec3(zs), state_conv[0], state_ssm[0], wts, t_valid=dseq)
    yd = yd[:, :dseq].reshape(nd, D_INNER)
    conv_p = xbcb.reshape(b, seq, -1)[:, seq - (CONV_WIDTH - 1):]
    xbcd = xbcs[:nd].reshape(db, dseq, -1)
    conv_s = jnp.concatenate([state_conv[0].astype(F32), xbcd], axis=1)[:, -(CONV_WIDTH - 1):]

    tq = _row_tile(seq, 512)
    cos_b, sin_b = _rope_tables(np.arange(seq, dtype=np.int32))
    pos_s = np.concatenate([np.tile(past + np.arange(dseq, dtype=np.int32), db),
                            np.arange(n_meta, dtype=np.int32) - n_meta])
    cos_s, sin_s = _rope_tables(pos_s)
    qb, ckvb, krb = _mla_prep(miscb, cos_b, sin_b, seq // tq, wts, tm=tq)
    qs, ckvs, krs = _mla_prep(miscs, cos_s, sin_s, 1, wts, tm=miscs.shape[0])
    kb, vb = _kv_expand(ckvb, krb, wts, tm=tq)
    kp, vp = _kv_expand(_pad_rows(ckvs[nd:], 0, LANES), _pad_rows(krs[nd:], 0, LANES), wts, tm=LANES)
    tkd = -(-(past + dseq) // LANES) * LANES
    ckv_d = _pad_rows(jnp.concatenate([cache_ckv[0].astype(F32), ckvs[:nd].reshape(db, dseq, -1)], axis=1), 1, tkd)
    kr_cache = jnp.pad(cache_krope[0].astype(F32),
                       ((0, 0), (0, 0), (QK_NOPE_DIM, LANES - QK_NOPE_DIM - QK_ROPE_DIM)))
    kr_d = _pad_rows(jnp.concatenate([kr_cache, krs[:nd].reshape(db, dseq, -1)], axis=1), 1, tkd)
    kd, vd = _kv_expand(ckv_d.reshape(db * tkd, -1), kr_d.reshape(db * tkd, -1), wts,
                        tm=_row_tile(db * tkd, 512))

    hsplit = lambda a, nb: a.reshape(MLA_HEADS, nb, -1, LANES)
    ob = _attention(hsplit(qb, b), kb, hsplit(vb, b), kp, vp,
                    causal=True, tq=_row_tile(seq, ATTN_TQ), tk=tq, prefix_valid=n_meta)
    od = _attention(hsplit(qs[:, :nd], db), kd, hsplit(vd, db), kp, vp,
                    causal=False, tq=dseq, tk=tkd, prefix_valid=n_meta, kv_valid=past + dseq)

    h2b = _mix(hb, yb.reshape(b * seq, -1), ob.reshape(HEAD_SLABS, b * seq, SLAB_W), gateb, wts)
    h2d = _mix(hs[:nd], yd, od.reshape(HEAD_SLABS, nd, SLAB_W), gates[:nd], wts)
    (y_p,) = _ffn(h2b, ffn2_norm[0], wts["ffn2_w_gu"], wts["ffn2_w_down"], final_norm, emit_h=False, out_dtype=F32)
    (y_s,) = _ffn(h2d, ffn2_norm[0], wts["ffn2_w_gu"], wts["ffn2_w_down"], final_norm, emit_h=False, out_dtype=F32)

    kr32 = lambda a: a[..., QK_NOPE_DIM:QK_NOPE_DIM + QK_ROPE_DIM]
    ckv_meta = jnp.broadcast_to(ckvs[nd:][None], (b, n_meta, KV_LORA))
    kr_meta = jnp.broadcast_to(kr32(krs[nd:])[None], (b, n_meta, QK_ROPE_DIM))
    ckv_p = jnp.concatenate([ckv_meta, ckvb.reshape(b, seq, -1)], axis=1)
    kr_p = jnp.concatenate([kr_meta, kr32(krb).reshape(b, seq, -1)], axis=1)
    return (y_p.reshape(b, seq, D_MODEL), y_s.reshape(db, dseq, D_MODEL),
            ckv_p[None], kr_p[None], ssm_p[None], conv_p[None],
            ckvs[:nd].reshape(db, dseq, -1)[None], kr32(krs[:nd]).reshape(db, dseq, -1)[None],
            ssm_s[None], conv_s[None])
```

```python
import functools
import math

import jax
import jax.numpy as jnp
from jax import lax
from jax.experimental import pallas as pl
from jax.experimental.pallas import tpu as pltpu

F32 = jnp.float32
BF16 = jnp.bfloat16

D_MODEL = 1024
CHUNK = 64
FFN_DIM = 2816
NORM_EPS = 1e-6
D_INNER = 2048
SSM_HEAD_DIM = 64
SSM_HEADS = 32
SSM_GROUPS = 4
SSM_STATE = 128
CONV_WIDTH = 4
CONV_DIM = D_INNER + 2 * SSM_GROUPS * SSM_STATE
MLA_HEADS = 16
HEAD_PACK = 4
HEAD_SLABS = MLA_HEADS // HEAD_PACK
Q_LORA = 384
KV_LORA = 256
QK_NOPE_DIM = 64
QK_ROPE_DIM = 32
V_HEAD_DIM = 64
SLAB_W = HEAD_PACK * V_HEAD_DIM
ROPE_THETA = 10000.0
OFF_Z = 0
OFF_XBC = OFF_Z + D_INNER
OFF_DT = OFF_XBC + CONV_DIM
OFF_CQ = OFF_DT + SSM_HEADS
OFF_CKV = OFF_CQ + Q_LORA
OFF_KROPE = OFF_CKV + KV_LORA
OFF_GATE = OFF_KROPE + QK_ROPE_DIM

LANES = 128
SSD_L = 128
FFN_TF = 256
HEAD_UNROLL = 8
ATTN_TQ = 1024
ATTN_TK = 1024
ATTN_DIAG_TK = 256
NEG_BIG = -1e30
VMEM_LIMIT = 52 * 1024 * 1024

MISC_CQ = 0
MISC_CKV = 384
MISC_KRA = 640
MISC_KRB = 768
MISC_DT = 896
MISC_W = 1024


def _cparams(n_grid):
    return pltpu.CompilerParams(dimension_semantics=("arbitrary",) * n_grid,
                                vmem_limit_bytes=VMEM_LIMIT)


def _row_tile(m, pref):
    if m <= pref:
        return m
    t = pref
    while t >= LANES:
        if m % t == 0:
            return t
        t //= 2
    return m


def _rms(x, g):
    return x * lax.rsqrt(jnp.mean(x * x, axis=-1, keepdims=True) + NORM_EPS) * g


def _dot(a, b):
    return jnp.dot(a, b, preferred_element_type=F32)


def _ffn_body(x_ref, g1_ref, wgu_ref, wd_ref, g2_ref, *outs, emit_h):
    x = x_ref[...]
    xn = _rms(x, g1_ref[...]).astype(BF16)
    acc = None
    for f in range(0, FFN_DIM, FFN_TF):
        g = _dot(xn, wgu_ref[:, f:f + FFN_TF])
        u = _dot(xn, wgu_ref[:, FFN_DIM + f:FFN_DIM + f + FFN_TF])
        act = (g * jax.nn.sigmoid(g) * u).astype(BF16)
        d = _dot(act, wd_ref[f:f + FFN_TF, :])
        acc = d if acc is None else acc + d
    h = x + 0.5 * acc
    if emit_h:
        outs[0][...] = h
    outs[-1][...] = _rms(h, g2_ref[...]).astype(outs[-1].dtype)


def _ffn(x, g1, w_gu, w_down, g2, *, emit_h, out_dtype, tm_pref=512):
    m = x.shape[0]
    tm = _row_tile(m, tm_pref)
    row = pl.BlockSpec((tm, D_MODEL), lambda i: (i, 0))
    vec = pl.BlockSpec((1, D_MODEL), lambda i: (0, 0))
    out_shape = [jax.ShapeDtypeStruct((m, D_MODEL), out_dtype)]
    out_specs = [row]
    if emit_h:
        out_shape = [jax.ShapeDtypeStruct((m, D_MODEL), F32)] + out_shape
        out_specs = [row, row]
    return pl.pallas_call(
        functools.partial(_ffn_body, emit_h=emit_h),
        out_shape=out_shape,
        grid=(m // tm,),
        in_specs=[row, vec,
                  pl.BlockSpec((D_MODEL, 2 * FFN_DIM), lambda i: (0, 0), pipeline_mode=pl.Buffered(1)),
                  pl.BlockSpec((FFN_DIM, D_MODEL), lambda i: (0, 0), pipeline_mode=pl.Buffered(1)),
                  vec],
        out_specs=out_specs,
        compiler_params=_cparams(1),
        name="ffn",
    )(x, g1.reshape(1, -1), w_gu, w_down, g2.reshape(1, -1))


def _softplus(x):
    return jnp.maximum(x, 0.0) + jnp.log1p(jnp.exp(-jnp.abs(x)))


def _expand_heads(x, e):
    hi = x.astype(BF16)
    lo = (x - hi.astype(F32)).astype(BF16)
    return _dot(hi, e) + _dot(lo, e)


def _ssd_body(xbc_ref, dtb_ref, zs_ref, convp_ref, ssmp_ref, cw_ref, cb_ref, dtbias_ref, alog_ref,
              dvec_ref, nrm_ref, e_ref, y_ref, ssm_ref, xpad, state, y_scr, *, t_valid):
    t = pl.program_id(1)
    L = SSD_L
    gs = SSM_STATE
    hg = SSM_HEADS // SSM_GROUPS
    gw = hg * SSM_HEAD_DIM

    @pl.when(t == 0)
    def _():
        xpad[0:8, :] = convp_ref[...]
        state[...] = ssmp_ref[...].T

    xpad[8:8 + L, :] = xbc_ref[...]

    def conv_act(c0, width):
        cs = slice(c0, c0 + width)
        acc = cb_ref[:, cs] + xpad[5:5 + L, cs] * cw_ref[0:1, cs]
        for k in range(1, CONV_WIDTH):
            acc = acc + xpad[5 + k:5 + k + L, cs] * cw_ref[k:k + 1, cs]
        return acc * jax.nn.sigmoid(acc)

    dt =_softplus(dtb_ref[:, 0:SSM_HEADS] + dtbias_ref[...])
    if t_valid is not None:
        rows = t * L + lax.broadcasted_iota(jnp.int32, (L, SSM_HEADS), 0)
        dt = jnp.where(rows < t_valid, dt, 0.0)
    da = dt * (-jnp.exp(alog_ref[...]))
    ri = lax.broadcasted_iota(jnp.int32, (L, L), 0)
    ci = lax.broadcasted_iota(jnp.int32, (L, L), 1)
    causal = ri >= ci
    a_cs = jnp.dot(causal.astype(F32), da, precision=lax.Precision.HIGHEST,
                   preferred_element_type=F32)
    eye = (lax.broadcasted_iota(jnp.int32, (SSM_HEADS, SSM_HEADS), 0)
           == lax.broadcasted_iota(jnp.int32, (SSM_HEADS, SSM_HEADS), 1)).astype(F32)
    a_cs_t = lax.dot_general(eye, a_cs, (((1,), (1,)), ((), ())), precision=lax.Precision.HIGHEST,
                             preferred_element_type=F32)
    total = a_cs[L - 1:L, :]

    exp_cs = jnp.exp(a_cs)
    w_in = jnp.exp(total - a_cs) * dt
    exp_tot = jnp.broadcast_to(jnp.exp(total), (8, SSM_HEADS))

    for g in range(SSM_GROUPS):
        gsl = slice(g * gw, (g + 1) * gw)
        xs = conv_act(g * gw, gw)
        bg = conv_act(D_INNER + g * gs, gs).astype(BF16)
        cg = conv_act(D_INNER + SSM_GROUPS * gs + g * gs, gs).astype(BF16)
        e = e_ref[:, gsl]
        xdt = xs * _expand_heads(dt, e)
        wx = (xs * _expand_heads(w_in, e)).astype(BF16)
        cb = lax.dot_general(cg, bg, (((1,), (1,)), ((), ())), preferred_element_type=F32)
        cb = jnp.where(causal, cb, 0.0)
        for hh in range(hg):
            h = g * hg + hh
            seg = a_cs[:, h:h + 1] - a_cs_t[h:h + 1, :]
            mh = (cb * jnp.exp(jnp.minimum(seg, 0.0))).astype(BF16)
            cs = slice(hh * SSM_HEAD_DIM, (hh + 1) * SSM_HEAD_DIM)
            y_scr[:, cs] = _dot(mh, xdt[:, cs].astype(BF16))
        st = state[:, gsl]
        y = y_scr[...] + _dot(cg, st.astype(BF16)) * _expand_heads(exp_cs, e) + dvec_ref[:, gsl] * xs
        upd = lax.dot_general(bg, wx, (((0,), (0,)), ((), ())), preferred_element_type=F32)
        state[:, gsl] = st * _expand_heads(exp_tot, e)[0:1, :] + upd
        y = y * zs_ref[:, gsl].astype(F32)
        y = y * lax.rsqrt(jnp.mean(y * y, axis=-1, keepdims=True) + NORM_EPS)
        y_ref[:, gsl] = (y * nrm_ref[:, gsl]).astype(y_ref.dtype)

    xpad[0:8, :] = xpad[L:L + 8, :]

    @pl.when(t == pl.num_programs(1) - 1)
    def _():
        ssm_ref[...] = state[...].T


def _ssd(xbc, misc, z, conv_prev, ssm_prev, wts, *, t_valid):
    b, t, _ = xbc.shape
    assert t % SSD_L == 0
    convp = jnp.pad(conv_prev.astype(F32), ((0, 0), (8 - (CONV_WIDTH - 1), 0), (0, 0)))
    ssmp = ssm_prev.astype(F32).reshape(b, D_INNER, SSM_STATE)
    tok = lambda w: pl.BlockSpec((None, SSD_L, w), lambda bi, ti: (bi, ti, 0))
    full = lambda r, c: pl.BlockSpec((r, c), lambda bi, ti: (0, 0))
    y, ssm = pl.pallas_call(
        functools.partial(_ssd_body, t_valid=t_valid),
        out_shape=[jax.ShapeDtypeStruct((b, t, D_INNER), BF16),
                   jax.ShapeDtypeStruct((b, D_INNER, SSM_STATE), F32)],
        grid=(b, t // SSD_L),
        in_specs=[tok(CONV_DIM),
                  pl.BlockSpec((None, SSD_L, LANES), lambda bi, ti: (bi, ti, MISC_DT // LANES)),
                  tok(D_INNER),
                  pl.BlockSpec((None, 8, CONV_DIM), lambda bi, ti: (bi, 0, 0)),
                  pl.BlockSpec((None, D_INNER, SSM_STATE), lambda bi, ti: (bi, 0, 0)),
                  full(CONV_WIDTH, CONV_DIM), full(1, CONV_DIM), full(1, SSM_HEADS), full(1, SSM_HEADS),
                  full(1, D_INNER), full(1, D_INNER), full(SSM_HEADS, D_INNER)],
        out_specs=[tok(D_INNER),
                   pl.BlockSpec((None, D_INNER, SSM_STATE), lambda bi, ti: (bi, 0, 0))],
        scratch_shapes=[pltpu.VMEM((SSD_L + 8, CONV_DIM), F32),
                        pltpu.VMEM((SSM_STATE, D_INNER), F32),
                        pltpu.VMEM((SSD_L, D_INNER // SSM_GROUPS), F32)],
        compiler_params=_cparams(2),
        name="ssd",
    )(xbc, misc, z, convp, ssmp, wts["conv_w"], wts["conv_b"], wts["dt_bias"], wts["a_log"],
      wts["dvec"], wts["ssm_norm"], wts["expand"])
    return y, ssm.reshape(b, SSM_HEADS, SSM_HEAD_DIM, SSM_STATE)


def _mla_prep_body(misc_ref, cos_ref, sin_ref, qn_ref, kvn_ref, w1_ref, w2_ref, q_ref, ckv_ref, kr_ref):
    cos = cos_ref[...]
    sin = sin_ref[...]
    qn = _rms(misc_ref[:, MISC_CQ:MISC_CQ + Q_LORA], qn_ref[...]).astype(BF16)
    q1 = _dot(qn, w1_ref[...])
    q2 = _dot(qn, w2_ref[...])
    scale = (QK_NOPE_DIM + QK_ROPE_DIM) ** -0.5 * math.log2(math.e)
    for h in range(MLA_HEADS):
        hs = slice(h * LANES, (h + 1) * LANES)
        q_ref[h] = ((q1[:, hs] * cos + q2[:, hs] * sin) * scale).astype(BF16)
    ckv_ref[...] = _rms(misc_ref[:, MISC_CKV:MISC_CKV + KV_LORA], kvn_ref[...])
    kr_ref[...] = misc_ref[:, MISC_KRA:MISC_KRA + LANES] * cos + misc_ref[:, MISC_KRB:MISC_KRB + LANES] * sin


def _mla_prep(misc, cos, sin, n_pos_blocks, wts, *, tm):
    m = misc.shape[0]
    assert m % tm == 0
    return pl.pallas_call(
        _mla_prep_body,
        out_shape=[jax.ShapeDtypeStruct((MLA_HEADS, m, LANES), BF16),
                   jax.ShapeDtypeStruct((m, KV_LORA), F32),
                   jax.ShapeDtypeStruct((m, LANES), F32)],
        grid=(m // tm,),
        in_specs=[pl.BlockSpec((tm, MISC_W), lambda i: (i, 0)),
                  pl.BlockSpec((tm, LANES), lambda i: (i % n_pos_blocks, 0)),
                  pl.BlockSpec((tm, LANES), lambda i: (i % n_pos_blocks, 0)),
                  pl.BlockSpec((1, Q_LORA), lambda i: (0, 0)),
                  pl.BlockSpec((1, KV_LORA), lambda i: (0, 0)),
                  pl.BlockSpec((Q_LORA, MLA_HEADS * LANES), lambda i: (0, 0)),
                  pl.BlockSpec((Q_LORA, MLA_HEADS * LANES), lambda i: (0, 0))],
        out_specs=[pl.BlockSpec((MLA_HEADS, tm, LANES), lambda i: (0, i, 0)),
                   pl.BlockSpec((tm, KV_LORA), lambda i: (i, 0)),
                   pl.BlockSpec((tm, LANES), lambda i: (i, 0))],
        compiler_params=_cparams(1),
        name="mla_prep",
    )(misc, cos, sin, wts["q_norm"], wts["kv_norm"], wts["w_q1"], wts["w_q2"])


def _kv_body(ckv_ref, kr_ref, wk_ref, wv_ref, one_ref, kt_ref, v_ref):
    c = ckv_ref[...].astype(BF16)
    k = _dot(c, wk_ref[...])
    v = _dot(c, wv_ref[...])
    kr = kr_ref[...]
    one = one_ref[...]
    for h in range(MLA_HEADS):
        hs = slice(h * LANES, (h + 1) * LANES)
        kt_ref[h] = (k[:, hs] + kr).T.astype(BF16)
        v_ref[h] = (v[:, hs] + one).astype(BF16)


def _kv_expand(ckv, kr128, wts, *, tm):
    m = ckv.shape[0]
    assert m % tm == 0
    hw = MLA_HEADS * LANES
    return pl.pallas_call(
        _kv_body,
        out_shape=[jax.ShapeDtypeStruct((MLA_HEADS, LANES, m), BF16),
                   jax.ShapeDtypeStruct((MLA_HEADS, m, LANES), BF16)],
        grid=(m // tm,),
        in_specs=[pl.BlockSpec((tm, KV_LORA), lambda i: (i, 0)),
                  pl.BlockSpec((tm, LANES), lambda i: (i, 0)),
                  pl.BlockSpec((KV_LORA, hw), lambda i: (0, 0)),
                  pl.BlockSpec((KV_LORA, hw), lambda i: (0, 0)),
                  pl.BlockSpec((1, LANES), lambda i: (0, 0))],
        out_specs=[pl.BlockSpec((MLA_HEADS, LANES, tm), lambda i: (0, 0, i)),
                   pl.BlockSpec((MLA_HEADS, tm, LANES), lambda i: (0, i, 0))],
        compiler_params=_cparams(1),
        name="kv_expand",
    )(ckv, kr128, wts["w_k"], wts["w_v"], wts["one_col"])


def _attn_body(q_ref, kt_ref, v_ref, kpt_ref, vp_ref, o_ref, *scr, causal, prefix_valid, kv_valid):
    acc_scrs, m_scrs = scr[:HEAD_UNROLL], scr[HEAD_UNROLL:]
    i = pl.program_id(1)
    j = pl.program_id(2)
    tq = q_ref.shape[1]
    tk = kt_ref.shape[2]
    tp = kpt_ref.shape[2]
    n_trips = MLA_HEADS // HEAD_UNROLL

    def step(ktr, vr, mask, row0=0, k0=0, w=None):
        w = ktr.shape[2] - k0 if w is None else w
        rows = pl.ds(row0, tq - row0)
        keys = pl.ds(k0, w)

        def body(hp, carry):
            for u in range(HEAD_UNROLL):
                h = hp * HEAD_UNROLL + u
                acc_scr, m_scr = acc_scrs[u], m_scrs[u]
                s = _dot(q_ref[h, rows, :], ktr[h, :, keys])
                if mask is not None:
                    s = jnp.where(mask, s, NEG_BIG)
                m_prev = m_scr[hp, rows, :]
                m_new = jnp.maximum(m_prev, jnp.max(s, axis=1, keepdims=True))
                alpha = jnp.exp2(m_prev - m_new)
                p = jnp.exp2(s - pltpu.repeat(m_new, w // LANES, 1)).astype(BF16)
                acc_scr[hp, rows, :] = acc_scr[hp, rows, :] * alpha + _dot(p, vr[h, keys, :])
                m_scr[hp, rows, :] = m_new
            return carry

        lax.fori_loop(0, n_trips, body, 0)

    def finalize():
        lane = lax.broadcasted_iota(jnp.int32, (tq, LANES), 1)

        def pair(u, hp):
            a = acc_scrs[u][hp]
            b = acc_scrs[u + 1][hp]
            oa = a / a[:, V_HEAD_DIM:V_HEAD_DIM + 1]
            ob = b / b[:, V_HEAD_DIM:V_HEAD_DIM + 1]
            return jnp.where(lane < V_HEAD_DIM, oa, pltpu.roll(ob, V_HEAD_DIM, 1))

        def body(hp, carry):
            for u in range(0, HEAD_UNROLL, HEAD_PACK):
                slab = jnp.concatenate([pair(u + 2 * k, hp) for k in range(HEAD_PACK // 2)], axis=1)
                o_ref[hp * (HEAD_UNROLL // HEAD_PACK) + u // HEAD_PACK] = slab.astype(o_ref.dtype)
            return carry

        lax.fori_loop(0, n_trips, body, 0)

    @pl.when(j == 0)
    def _():
        for u in range(HEAD_UNROLL):
            acc_scrs[u][...] = jnp.zeros_like(acc_scrs[u])
            m_scrs[u][...] = jnp.full_like(m_scrs[u], NEG_BIG)
        step(kpt_ref, vp_ref, lax.broadcasted_iota(jnp.int32, (tq, tp), 1) < prefix_valid)

    if causal:
        r = tq // tk
        @pl.when(j < r * i)
        def _():
            step(kt_ref, v_ref, None)

        sub = min(tk, ATTN_DIAG_TK)
        for jj in range(r):
            @pl.when(j == r * i + jj)
            def _(jj=jj):
                for k0 in range(0, tk, sub):
                    row0 = jj * tk + k0
                    qc = (i * tq + row0 + lax.broadcasted_iota(jnp.int32, (tq - row0, sub), 0)) // CHUNK
                    kc = (j * tk + k0 + lax.broadcasted_iota(jnp.int32, (tq - row0, sub), 1)) // CHUNK
                    step(kt_ref, v_ref, kc <= qc, row0, k0, sub)

        @pl.when(j == r * (i + 1) - 1)
        def _():
            finalize()
    else:
        mask = None
        if kv_valid is not None:
            mask = (j * tk + lax.broadcasted_iota(jnp.int32, (tq, tk), 1)) < kv_valid
        step(kt_ref, v_ref, mask)

        @pl.when(j == pl.num_programs(2) - 1)
        def _():
            finalize()


def _attention(q, kt, v, kpt, vp, *, causal, tq, tk, prefix_valid, kv_valid=None):
    hh, b, t, _ = q.shape
    tkv = v.shape[2]
    assert t % tq == 0 and tkv % tk == 0
    nkb = tkv // tk
    if causal:
        assert tq % tk == 0 and t == tkv and tk % CHUNK == 0
        r = tq // tk
        kblk = lambda i, j: jnp.minimum(r * (i + 1) - 1, j)
    else:
        kblk = lambda i, j: j
    tp = vp.shape[1]
    return pl.pallas_call(
        functools.partial(_attn_body, causal=causal, prefix_valid=prefix_valid, kv_valid=kv_valid),
        out_shape=jax.ShapeDtypeStruct((HEAD_SLABS, b, t, SLAB_W), BF16),
        grid=(b, t // tq, nkb),
        in_specs=[pl.BlockSpec((hh, None, tq, LANES), lambda bi, i, j: (0, bi, i, 0)),
                  pl.BlockSpec((hh, LANES, tk), lambda bi, i, j: (0, 0, bi * nkb + kblk(i, j))),
                  pl.BlockSpec((hh, None, tk, LANES), lambda bi, i, j: (0, bi, kblk(i, j), 0)),
                  pl.BlockSpec((hh, LANES, tp), lambda bi, i, j: (0, 0, 0)),
                  pl.BlockSpec((hh, tp, LANES), lambda bi, i, j: (0, 0, 0))],
        out_specs=pl.BlockSpec((HEAD_SLABS, None, tq, SLAB_W), lambda bi, i, j: (0, bi, i, 0)),
        scratch_shapes=[pltpu.VMEM((hh // HEAD_UNROLL, tq, LANES), F32)] * (2 * HEAD_UNROLL),
        compiler_params=_cparams(3),
        name="attention",
    )(q, kt, v, kpt, vp)


def _mix_body(h_ref, y_ref, o_ref, gate_ref, ws_ref, wm_ref, wx_ref, out_ref):
    y_s = _dot(y_ref[...], ws_ref[...])
    y_m = _dot(o_ref[0], wm_ref[0])
    for hp in range(1, HEAD_SLABS):
        y_m = y_m + _dot(o_ref[hp], wm_ref[hp])
    merged = gate_ref[:, :D_MODEL].astype(F32) * y_s + gate_ref[:, D_MODEL:].astype(F32) * y_m
    out_ref[...] = h_ref[...] + _dot(merged.astype(BF16), wx_ref[...])


def _mix(h, y, o, gate, wts, *, tm_pref=256):
    m = h.shape[0]
    tm = _row_tile(m, tm_pref)
    return pl.pallas_call(
        _mix_body,
        out_shape=jax.ShapeDtypeStruct((m, D_MODEL), F32),
        grid=(m // tm,),
        in_specs=[pl.BlockSpec((tm, D_MODEL), lambda i: (i, 0)),
                  pl.BlockSpec((tm, D_INNER), lambda i: (i, 0)),
                  pl.BlockSpec((HEAD_SLABS, tm, SLAB_W), lambda i: (0, i, 0)),
                  pl.BlockSpec((tm, 2 * D_MODEL), lambda i: (i, 0)),
                  pl.BlockSpec((D_INNER, D_MODEL), lambda i: (0, 0)),
                  pl.BlockSpec((HEAD_SLABS, SLAB_W, D_MODEL), lambda i: (0, 0, 0)),
                  pl.BlockSpec((D_MODEL, D_MODEL), lambda i: (0, 0))],
        out_specs=pl.BlockSpec((tm, D_MODEL), lambda i: (i, 0)),
        compiler_params=_cparams(1),
        name="mix",
    )(h, y, o, gate, wts["w_ssm_out"], wts["w_mla_out"], wts["w_mix_out"])


def _prep_weights(p):
    w_in = p["w_in"]
    w_kr = w_in[:, OFF_KROPE:OFF_GATE]
    half = QK_ROPE_DIM // 2
    w_kr_sw = jnp.concatenate([w_kr[:, half:], w_kr[:, :half]], axis=1)
    zc = lambda n: jnp.zeros((D_MODEL, n), F32)
    w_misc = jnp.concatenate([
        w_in[:, OFF_CQ:OFF_CKV], w_in[:, OFF_CKV:OFF_KROPE],
        zc(QK_NOPE_DIM), w_kr, zc(LANES - QK_NOPE_DIM - QK_ROPE_DIM),
        zc(QK_NOPE_DIM), w_kr_sw, zc(LANES - QK_NOPE_DIM - QK_ROPE_DIM),
        w_in[:, OFF_DT:OFF_CQ], zc(LANES - SSM_HEADS)], axis=1)

    wq = p["w_uq"].reshape(Q_LORA, MLA_HEADS, QK_NOPE_DIM + QK_ROPE_DIM)
    q_nope, q_rope = wq[..., :QK_NOPE_DIM], wq[..., QK_NOPE_DIM:]
    q_rope_sw = jnp.concatenate([q_rope[..., half:], q_rope[..., :half]], axis=-1)
    zq = lambda n: jnp.zeros((Q_LORA, MLA_HEADS, n), F32)
    pad = LANES - QK_NOPE_DIM - QK_ROPE_DIM
    w_q1 = jnp.concatenate([q_nope, q_rope, zq(pad)], axis=-1).reshape(Q_LORA, MLA_HEADS * LANES)
    w_q2 = jnp.concatenate([zq(QK_NOPE_DIM), q_rope_sw, zq(pad)], axis=-1).reshape(Q_LORA, MLA_HEADS * LANES)

    wkv = p["w_ukv"].reshape(KV_LORA, MLA_HEADS, QK_NOPE_DIM + V_HEAD_DIM)
    zk = jnp.zeros((KV_LORA, MLA_HEADS, LANES - QK_NOPE_DIM), F32)
    w_k = jnp.concatenate([wkv[..., :QK_NOPE_DIM], zk], axis=-1).reshape(KV_LORA, MLA_HEADS * LANES)
    zv = jnp.zeros((KV_LORA, MLA_HEADS, LANES - V_HEAD_DIM), F32)
    w_v = jnp.concatenate([wkv[..., QK_NOPE_DIM:], zv], axis=-1).reshape(KV_LORA, MLA_HEADS * LANES)

    w_mla_out = p["mla_w_out"].reshape(HEAD_SLABS, SLAB_W, D_MODEL)

    expand = jnp.repeat(jnp.eye(SSM_HEADS, dtype=F32), SSM_HEAD_DIM, axis=1)
    one_col = (jnp.arange(LANES) == V_HEAD_DIM).astype(F32).reshape(1, LANES)
    return {
        "w_z": w_in[:, OFF_Z:OFF_XBC].astype(BF16),
        "w_xbc": w_in[:, OFF_XBC:OFF_DT].astype(BF16),
        "w_gate": w_in[:, OFF_GATE:].astype(BF16),
        "w_misc": w_misc.astype(BF16),
        "w_q1": w_q1.astype(BF16), "w_q2": w_q2.astype(BF16),
        "w_k": w_k.astype(BF16), "w_v": w_v.astype(BF16),
        "w_mla_out": w_mla_out.astype(BF16),
        "w_ssm_out": p["ssm_w_out"].astype(BF16),
        "w_mix_out": p["w_mix_out"].astype(BF16),
        "ffn1_w_gu": p["ffn1_w_gu"].astype(BF16), "ffn1_w_down": p["ffn1_w_down"].astype(BF16),
        "ffn2_w_gu": p["ffn2_w_gu"].astype(BF16), "ffn2_w_down": p["ffn2_w_down"].astype(BF16),
        "gate_bias": p["gate_bias"].reshape(1, -1),
        "conv_w": p["conv_w"], "conv_b": p["conv_b"].reshape(1, -1),
        "dt_bias": p["dt_bias"].reshape(1, -1), "a_log": p["a_log"].reshape(1, -1),
        "dvec": jnp.repeat(p["ssm_d"], SSM_HEAD_DIM).reshape(1, -1),
        "ssm_norm": p["ssm_norm"].reshape(1, -1),
        "expand": expand.astype(BF16), "one_col": one_col,
        "q_norm": p["q_norm"].reshape(1, -1), "kv_norm": p["kv_norm"].reshape(1, -1),
    }


def _rope_tables(pos):
    half = QK_ROPE_DIM // 2
    inv = ROPE_THETA ** (-jnp.arange(half, dtype=F32) / half)
    ang = pos.astype(F32)[:, None] * inv[None, :]
    c, s = jnp.cos(ang), jnp.sin(ang)
    t = pos.shape[0]
    pad = jnp.zeros((t, LANES - QK_NOPE_DIM - QK_ROPE_DIM), F32)
    cos = jnp.concatenate([jnp.ones((t, QK_NOPE_DIM), F32), c, c, pad], axis=1)
    sin = jnp.concatenate([jnp.zeros((t, QK_NOPE_DIM), F32), -s, s, pad], axis=1)
    return cos, sin


def _pad_rows(a, axis, to):
    padw = [(0, 0)] * a.ndim
    padw[axis] = (0, to - a.shape[axis])
    return jnp.pad(a, padw)


PROJ_CW = 512


def _in_proj_body(u_ref, wz_ref, wx_ref, wg_ref, wm_ref, gb_ref, zs_ref, xbc_ref, gate_ref, misc_ref):
    x = u_ref[...]
    for c in range(0, D_INNER, PROJ_CW):
        cs = slice(c, c + PROJ_CW)
        z = _dot(x, wz_ref[:, cs])
        zs_ref[:, cs] = (z * jax.nn.sigmoid(z)).astype(zs_ref.dtype)
    for c in range(0, CONV_DIM, PROJ_CW):
        cs = slice(c, c + PROJ_CW)
        xbc_ref[:, cs] = _dot(x, wx_ref[:, cs])
    for c in range(0, 2 * D_MODEL, PROJ_CW):
        cs = slice(c, c + PROJ_CW)
        gate_ref[:, cs] = jax.nn.sigmoid(_dot(x, wg_ref[:, cs]) + gb_ref[:, cs]).astype(gate_ref.dtype)
    for c in range(0, MISC_W, PROJ_CW):
        cs = slice(c, c + PROJ_CW)
        misc_ref[:, cs] = _dot(x, wm_ref[:, cs])


def _in_proj(u, wts, *, tm_pref=512):
    m = u.shape[0]
    tm = _row_tile(m, tm_pref)
    widths = (D_INNER, CONV_DIM, 2 * D_MODEL, MISC_W)
    dtypes = (BF16, F32, BF16, F32)
    resident = lambda n: pl.BlockSpec((D_MODEL, n), lambda i: (0, 0), pipeline_mode=pl.Buffered(1))
    return pl.pallas_call(
        _in_proj_body,
        out_shape=[jax.ShapeDtypeStruct((m, n), dt) for n, dt in zip(widths, dtypes)],
        grid=(m // tm,),
        in_specs=[pl.BlockSpec((tm, D_MODEL), lambda i: (i, 0))] + [resident(n) for n in widths]
                 + [pl.BlockSpec((1, 2 * D_MODEL), lambda i: (0, 0))],
        out_specs=[pl.BlockSpec((tm, n), lambda i: (i, 0)) for n in widths],
        compiler_params=_cparams(1),
        name="in_proj",
    )(u, wts["w_z"], wts["w_xbc"], wts["w_gate"], wts["w_misc"], wts["gate_bias"])


def kernel(x_prompt, x_sample, cache_ckv, cache_krope, state_ssm, state_conv, meta_tokens, ffn1_norm, ffn1_w_gu, ffn1_w_down, mix_norm, w_in, gate_bias, conv_w, conv_b, dt_bias, a_log, ssm_d, ssm_norm, ssm_w_out, q_norm, w_uq, kv_norm, w_ukv, mla_w_out, w_mix_out, ffn2_norm, ffn2_w_gu, ffn2_w_down, final_norm):
    b, seq, _ = x_prompt.shape
    db, dseq, _ = x_sample.shape
    past = cache_ckv.shape[2]
    n_meta = meta_tokens.shape[0]
    nd = db * dseq
    wts = _prep_weights({
        "w_in": w_in[0], "w_uq": w_uq[0], "w_ukv": w_ukv[0], "mla_w_out": mla_w_out[0],
        "ssm_w_out": ssm_w_out[0], "w_mix_out": w_mix_out[0],
        "ffn1_w_gu": ffn1_w_gu[0], "ffn1_w_down": ffn1_w_down[0],
        "ffn2_w_gu": ffn2_w_gu[0], "ffn2_w_down": ffn2_w_down[0],
        "gate_bias": gate_bias[0], "conv_w": conv_w[0], "conv_b": conv_b[0], "dt_bias": dt_bias[0],
        "a_log": a_log[0], "ssm_d": ssm_d[0], "ssm_norm": ssm_norm[0],
        "q_norm": q_norm[0], "kv_norm": kv_norm[0]})

    xb = x_prompt.reshape(b * seq, D_MODEL)
    xs = jnp.concatenate([x_sample.reshape(nd, D_MODEL), meta_tokens.astype(F32)], axis=0)

    hb, ub = _ffn(xb, ffn1_norm[0], wts["ffn1_w_gu"], wts["ffn1_w_down"], mix_norm[0], emit_h=True, out_dtype=BF16)
    hs, us = _ffn(xs, ffn1_norm[0], wts["ffn1_w_gu"], wts["ffn1_w_down"], mix_norm[0], emit_h=True, out_dtype=BF16)
    zb, xbcb, gateb, miscb = _in_proj(ub, wts)
    zs, xbcs, gates, miscs = _in_proj(us, wts)

    L = SSD_L
    meta3 = lambda a: _pad_rows(a[nd:][None], 1, L)
    _, ssm_meta = _ssd(meta3(xbcs), meta3(miscs), meta3(zs),
                       jnp.zeros((1, CONV_WIDTH - 1, CONV_DIM), F32),
                       jnp.zeros((1, SSM_HEADS, SSM_HEAD_DIM, SSM_STATE), F32), wts, t_valid=n_meta)
    conv_meta = xbcs[nd + n_meta - (CONV_WIDTH - 1):nd + n_meta][None]
    yb, ssm_p = _ssd(xbcb.reshape(b, seq, -1), miscb.reshape(b, seq, -1), zb.reshape(b, seq, -1),
                     jnp.broadcast_to(conv_meta, (b,) + conv_meta.shape[1:]),
                     jnp.broadcast_to(ssm_meta, (b,) + ssm_meta.shape[1:]), wts, t_valid=None)
    dec3 = lambda a: _pad_rows(a[:nd].reshape(db, dseq, -1), 1, L)
    yd, ssm_s = _ssd(dec3(xbcs), dec3(miscs), dec3(zs), state_conv[0], state_ssm[0], wts, t_valid=dseq)
    yd = yd[:, :dseq].reshape(nd, D_INNER)
    conv_p = xbcb.reshape(b, seq, -1)[:, seq - (CONV_WIDTH - 1):]
    xbcd = xbcs[:nd].reshape(db, dseq, -1)
    conv_s = jnp.concatenate([state_conv[0].astype(F32), xbcd], axis=1)[:, -(CONV_WIDTH - 1):]

    tq = _row_tile(seq, 512)
    cos_b, sin_b = _rope_tables(jnp.arange(seq, dtype=jnp.int32))
    pos_s = jnp.concatenate([jnp.tile(past + jnp.arange(dseq, dtype=jnp.int32), db),
                             jnp.arange(n_meta, dtype=jnp.int32) - n_meta])
    cos_s, sin_s = _rope_tables(pos_s)
    qb, ckvb, krb = _mla_prep(miscb, cos_b, sin_b, seq // tq, wts, tm=tq)
    qs, ckvs, krs = _mla_prep(miscs, cos_s, sin_s, 1, wts, tm=miscs.shape[0])
    kb, vb = _kv_expand(ckvb, krb, wts, tm=tq)
    kp, vp = _kv_expand(_pad_rows(ckvs[nd:], 0, LANES), _pad_rows(krs[nd:], 0, LANES), wts, tm=LANES)
    tkd = -(-(past + dseq) // LANES) * LANES
    ckv_d = _pad_rows(jnp.concatenate([cache_ckv[0].astype(F32), ckvs[:nd].reshape(db, dseq, -1)], axis=1), 1, tkd)
    kr_cache = jnp.pad(cache_krope[0].astype(F32),
                       ((0, 0), (0, 0), (QK_NOPE_DIM, LANES - QK_NOPE_DIM - QK_ROPE_DIM)))
    kr_d = _pad_rows(jnp.concatenate([kr_cache, krs[:nd].reshape(db, dseq, -1)], axis=1), 1, tkd)
    kd, vd = _kv_expand(ckv_d.reshape(db * tkd, -1), kr_d.reshape(db * tkd, -1), wts,
                        tm=_row_tile(db * tkd, 512))

    hsplit = lambda a, nb: a.reshape(MLA_HEADS, nb, -1, LANES)
    ob = _attention(hsplit(qb, b), kb, hsplit(vb, b), kp, vp,
                    causal=True, tq=_row_tile(seq, ATTN_TQ), tk=_row_tile(seq, ATTN_TK), prefix_valid=n_meta)
    od = _attention(hsplit(qs[:, :nd], db), kd, hsplit(vd, db), kp, vp,
                    causal=False, tq=dseq, tk=tkd, prefix_valid=n_meta, kv_valid=past + dseq)

    h2b = _mix(hb, yb.reshape(b * seq, -1), ob.reshape(HEAD_SLABS, b * seq, SLAB_W), gateb, wts)
    h2d = _mix(hs[:nd], yd, od.reshape(HEAD_SLABS, nd, SLAB_W), gates[:nd], wts)
    (y_p,) = _ffn(h2b, ffn2_norm[0], wts["ffn2_w_gu"], wts["ffn2_w_down"], final_norm, emit_h=False, out_dtype=F32)
    (y_s,) = _ffn(h2d, ffn2_norm[0], wts["ffn2_w_gu"], wts["ffn2_w_down"], final_norm, emit_h=False, out_dtype=F32)

    kr32 = lambda a: a[..., QK_NOPE_DIM:QK_NOPE_DIM + QK_ROPE_DIM]
    ckv_meta = jnp.broadcast_to(ckvs[nd:][None], (b, n_meta, KV_LORA))
    kr_meta = jnp.broadcast_to(kr32(krs[nd:])[None], (b, n_meta, QK_ROPE_DIM))
    ckv_p = jnp.concatenate([ckv_meta, ckvb.reshape(b, seq, -1)], axis=1)
    kr_p = jnp.concatenate([kr_meta, kr32(krb).reshape(b, seq, -1)], axis=1)
    return (y_p.reshape(b, seq, D_MODEL), y_s.reshape(db, dseq, D_MODEL),
            ckv_p[None], kr_p[None], ssm_p[None], conv_p[None],
            ckvs[:nd].reshape(db, dseq, -1)[None], kr32(krs[:nd]).reshape(db, dseq, -1)[None],
            ssm_s[None], conv_s[None])
```

```python
import functools
import math

import jax
import jax.numpy as jnp
from jax import lax
from jax.experimental import pallas as pl
from jax.experimental.pallas import tpu as pltpu

F32 = jnp.float32
BF16 = jnp.bfloat16

D_MODEL = 1024
CHUNK = 64
FFN_DIM = 2816
NORM_EPS = 1e-6
D_INNER = 2048
SSM_HEAD_DIM = 64
SSM_HEADS = 32
SSM_GROUPS = 4
SSM_STATE = 128
CONV_WIDTH = 4
CONV_DIM = D_INNER + 2 * SSM_GROUPS * SSM_STATE
MLA_HEADS = 16
HEAD_PACK = 4
HEAD_SLABS = MLA_HEADS // HEAD_PACK
Q_LORA = 384
KV_LORA = 256
QK_NOPE_DIM = 64
QK_ROPE_DIM = 32
V_HEAD_DIM = 64
SLAB_W = HEAD_PACK * V_HEAD_DIM
ROPE_THETA = 10000.0
OFF_Z = 0
OFF_XBC = OFF_Z + D_INNER
OFF_DT = OFF_XBC + CONV_DIM
OFF_CQ = OFF_DT + SSM_HEADS
OFF_CKV = OFF_CQ + Q_LORA
OFF_KROPE = OFF_CKV + KV_LORA
OFF_GATE = OFF_KROPE + QK_ROPE_DIM

LANES = 128
SSD_L = 128
FFN_TF = 256
HEAD_UNROLL = 8
ATTN_TQ = 1024
ATTN_TK = 1024
ATTN_DIAG_TK = 256
NEG_BIG = -1e30
VMEM_LIMIT = 52 * 1024 * 1024

MISC_CQ = 0
MISC_CKV = 384
MISC_KRA = 640
MISC_KRB = 768
MISC_DT = 896
MISC_W = 1024


def _cparams(n_grid):
    return pltpu.CompilerParams(dimension_semantics=("arbitrary",) * n_grid,
                                vmem_limit_bytes=VMEM_LIMIT)


def _row_tile(m, pref):
    if m <= pref:
        return m
    t = pref
    while t >= LANES:
        if m % t == 0:
            return t
        t //= 2
    return m


def _rms(x, g):
    return x * lax.rsqrt(jnp.mean(x * x, axis=-1, keepdims=True) + NORM_EPS) * g


def _dot(a, b):
    return jnp.dot(a, b, preferred_element_type=F32)


def _ffn_body(x_ref, g1_ref, wgu_ref, wd_ref, g2_ref, *outs, emit_h):
    x = x_ref[...]
    xn = _rms(x, g1_ref[...]).astype(BF16)
    acc = None
    for f in range(0, FFN_DIM, FFN_TF):
        g = _dot(xn, wgu_ref[:, f:f + FFN_TF])
        u = _dot(xn, wgu_ref[:, FFN_DIM + f:FFN_DIM + f + FFN_TF])
        act = (g * jax.nn.sigmoid(g) * u).astype(BF16)
        d = _dot(act, wd_ref[f:f + FFN_TF, :])
        acc = d if acc is None else acc + d
    h = x + 0.5 * acc
    if emit_h:
        outs[0][...] = h
    outs[-1][...] = _rms(h, g2_ref[...]).astype(outs[-1].dtype)


def _ffn(x, g1, w_gu, w_down, g2, *, emit_h, out_dtype, tm_pref=512):
    m = x.shape[0]
    tm = _row_tile(m, tm_pref)
    row = pl.BlockSpec((tm, D_MODEL), lambda i: (i, 0))
    vec = pl.BlockSpec((1, D_MODEL), lambda i: (0, 0))
    out_shape = [jax.ShapeDtypeStruct((m, D_MODEL), out_dtype)]
    out_specs = [row]
    if emit_h:
        out_shape = [jax.ShapeDtypeStruct((m, D_MODEL), F32)] + out_shape
        out_specs = [row, row]
    return pl.pallas_call(
        functools.partial(_ffn_body, emit_h=emit_h),
        out_shape=out_shape,
        grid=(m // tm,),
        in_specs=[row, vec,
                  pl.BlockSpec((D_MODEL, 2 * FFN_DIM), lambda i: (0, 0), pipeline_mode=pl.Buffered(1)),
                  pl.BlockSpec((FFN_DIM, D_MODEL), lambda i: (0, 0), pipeline_mode=pl.Buffered(1)),
                  vec],
        out_specs=out_specs,
        compiler_params=_cparams(1),
        name="ffn",
    )(x, g1.reshape(1, -1), w_gu, w_down, g2.reshape(1, -1))


def _softplus(x):
    return jnp.maximum(x, 0.0) + jnp.log1p(jnp.exp(-jnp.abs(x)))


def _expand_heads(x, e):
    hi = x.astype(BF16)
    lo = (x - hi.astype(F32)).astype(BF16)
    return _dot(hi, e) + _dot(lo, e)


def _ssd_body(xbc_ref, dtb_ref, zs_ref, convp_ref, ssmp_ref, cw_ref, cb_ref, dtbias_ref, alog_ref,
              dvec_ref, nrm_ref, e_ref, y_ref, ssm_ref, xpad, state, y_scr, *, t_valid):
    t = pl.program_id(1)
    L = xbc_ref.shape[0]
    gs = SSM_STATE
    hg = SSM_HEADS // SSM_GROUPS
    gw = hg * SSM_HEAD_DIM

    @pl.when(t == 0)
    def _():
        xpad[0:8, :] = convp_ref[...]
        state[...] = ssmp_ref[...].T

    xpad[8:8 + L, :] = xbc_ref[...]

    def conv_act(c0, width):
        cs = slice(c0, c0 + width)
        acc = cb_ref[:, cs] + xpad[5:5 + L, cs] * cw_ref[0:1, cs]
        for k in range(1, CONV_WIDTH):
            acc = acc + xpad[5 + k:5 + k + L, cs] * cw_ref[k:k + 1, cs]
        return acc * jax.nn.sigmoid(acc)

    dt =_softplus(dtb_ref[:, 0:SSM_HEADS] + dtbias_ref[...])
    if t_valid is not None:
        rows = t * L + lax.broadcasted_iota(jnp.int32, (L, SSM_HEADS), 0)
        dt = jnp.where(rows < t_valid, dt, 0.0)
    da = dt * (-jnp.exp(alog_ref[...]))
    ri = lax.broadcasted_iota(jnp.int32, (L, L), 0)
    ci = lax.broadcasted_iota(jnp.int32, (L, L), 1)
    causal = ri >= ci
    a_cs = jnp.dot(causal.astype(F32), da, precision=lax.Precision.HIGHEST,
                   preferred_element_type=F32)
    eye = (lax.broadcasted_iota(jnp.int32, (SSM_HEADS, SSM_HEADS), 0)
           == lax.broadcasted_iota(jnp.int32, (SSM_HEADS, SSM_HEADS), 1)).astype(F32)
    a_cs_t = lax.dot_general(eye, a_cs, (((1,), (1,)), ((), ())), precision=lax.Precision.HIGHEST,
                             preferred_element_type=F32)
    total = a_cs[L - 1:L, :]

    exp_cs = jnp.exp(a_cs)
    w_in = jnp.exp(total - a_cs) * dt
    exp_tot = jnp.broadcast_to(jnp.exp(total), (8, SSM_HEADS))

    for g in range(SSM_GROUPS):
        gsl = slice(g * gw, (g + 1) * gw)
        xs = conv_act(g * gw, gw)
        bg = conv_act(D_INNER + g * gs, gs).astype(BF16)
        cg = conv_act(D_INNER + SSM_GROUPS * gs + g * gs, gs).astype(BF16)
        e = e_ref[:, gsl]
        xdt = xs * _expand_heads(dt, e)
        wx = (xs * _expand_heads(w_in, e)).astype(BF16)
        cb = lax.dot_general(cg, bg, (((1,), (1,)), ((), ())), preferred_element_type=F32)
        cb = jnp.where(causal, cb, 0.0)
        for hh in range(hg):
            h = g * hg + hh
            seg = a_cs[:, h:h + 1] - a_cs_t[h:h + 1, :]
            mh = (cb * jnp.exp(jnp.minimum(seg, 0.0))).astype(BF16)
            cs = slice(hh * SSM_HEAD_DIM, (hh + 1) * SSM_HEAD_DIM)
            y_scr[:, cs] = _dot(mh, xdt[:, cs].astype(BF16))
        st = state[:, gsl]
        y = y_scr[...] + _dot(cg, st.astype(BF16)) * _expand_heads(exp_cs, e) + dvec_ref[:, gsl] * xs
        upd = lax.dot_general(bg, wx, (((0,), (0,)), ((), ())), preferred_element_type=F32)
        state[:, gsl] = st * _expand_heads(exp_tot, e)[0:1, :] + upd
        y = y * zs_ref[:, gsl].astype(F32)
        y = y * lax.rsqrt(jnp.mean(y * y, axis=-1, keepdims=True) + NORM_EPS)
        y_ref[:, gsl] = (y * nrm_ref[:, gsl]).astype(y_ref.dtype)

    xpad[0:8, :] = xpad[L:L + 8, :]

    @pl.when(t == pl.num_programs(1) - 1)
    def _():
        ssm_ref[...] = state[...].T


def _ssd(xbc, misc, z, conv_prev, ssm_prev, wts, *, t_valid):
    b, t, _ = xbc.shape
    L = SSD_L if t % SSD_L == 0 else t
    assert t % L == 0 and L % 8 == 0 and L <= SSD_L
    convp = jnp.pad(conv_prev.astype(F32), ((0, 0), (8 - (CONV_WIDTH - 1), 0), (0, 0)))
    ssmp = ssm_prev.astype(F32).reshape(b, D_INNER, SSM_STATE)
    tok = lambda w: pl.BlockSpec((None, L, w), lambda bi, ti: (bi, ti, 0))
    full = lambda r, c: pl.BlockSpec((r, c), lambda bi, ti: (0, 0))
    y, ssm = pl.pallas_call(
        functools.partial(_ssd_body, t_valid=t_valid),
        out_shape=[jax.ShapeDtypeStruct((b, t, D_INNER), BF16),
                   jax.ShapeDtypeStruct((b, D_INNER, SSM_STATE), F32)],
        grid=(b, t // L),
        in_specs=[tok(CONV_DIM),
                  pl.BlockSpec((None, L, LANES), lambda bi, ti: (bi, ti, MISC_DT // LANES)),
                  tok(D_INNER),
                  pl.BlockSpec((None, 8, CONV_DIM), lambda bi, ti: (bi, 0, 0)),
                  pl.BlockSpec((None, D_INNER, SSM_STATE), lambda bi, ti: (bi, 0, 0)),
                  full(CONV_WIDTH, CONV_DIM), full(1, CONV_DIM), full(1, SSM_HEADS), full(1, SSM_HEADS),
                  full(1, D_INNER), full(1, D_INNER), full(SSM_HEADS, D_INNER)],
        out_specs=[tok(D_INNER),
                   pl.BlockSpec((None, D_INNER, SSM_STATE), lambda bi, ti: (bi, 0, 0))],
        scratch_shapes=[pltpu.VMEM((L + 8, CONV_DIM), F32),
                        pltpu.VMEM((SSM_STATE, D_INNER), F32),
                        pltpu.VMEM((L, D_INNER // SSM_GROUPS), F32)],
        compiler_params=_cparams(2),
        name="ssd",
    )(xbc, misc, z, convp, ssmp, wts["conv_w"], wts["conv_b"], wts["dt_bias"], wts["a_log"],
      wts["dvec"], wts["ssm_norm"], wts["expand"])
    return y, ssm.reshape(b, SSM_HEADS, SSM_HEAD_DIM, SSM_STATE)


def _mla_prep_body(misc_ref, cos_ref, sin_ref, qn_ref, kvn_ref, w1_ref, w2_ref, q_ref, ckv_ref, kr_ref):
    cos = cos_ref[...]
    sin = sin_ref[...]
    qn = _rms(misc_ref[:, MISC_CQ:MISC_CQ + Q_LORA], qn_ref[...]).astype(BF16)
    q1 = _dot(qn, w1_ref[...])
    q2 = _dot(qn, w2_ref[...])
    scale = (QK_NOPE_DIM + QK_ROPE_DIM) ** -0.5 * math.log2(math.e)
    for h in range(MLA_HEADS):
        hs = slice(h * LANES, (h + 1) * LANES)
        q_ref[h] = ((q1[:, hs] * cos + q2[:, hs] * sin) * scale).astype(BF16)
    ckv_ref[...] = _rms(misc_ref[:, MISC_CKV:MISC_CKV + KV_LORA], kvn_ref[...])
    kr_ref[...] = misc_ref[:, MISC_KRA:MISC_KRA + LANES] * cos + misc_ref[:, MISC_KRB:MISC_KRB + LANES] * sin


def _mla_prep(misc, cos, sin, n_pos_blocks, wts, *, tm):
    m = misc.shape[0]
    assert m % tm == 0
    return pl.pallas_call(
        _mla_prep_body,
        out_shape=[jax.ShapeDtypeStruct((MLA_HEADS, m, LANES), BF16),
                   jax.ShapeDtypeStruct((m, KV_LORA), F32),
                   jax.ShapeDtypeStruct((m, LANES), F32)],
        grid=(m // tm,),
        in_specs=[pl.BlockSpec((tm, MISC_W), lambda i: (i, 0)),
                  pl.BlockSpec((tm, LANES), lambda i: (i % n_pos_blocks, 0)),
                  pl.BlockSpec((tm, LANES), lambda i: (i % n_pos_blocks, 0)),
                  pl.BlockSpec((1, Q_LORA), lambda i: (0, 0)),
                  pl.BlockSpec((1, KV_LORA), lambda i: (0, 0)),
                  pl.BlockSpec((Q_LORA, MLA_HEADS * LANES), lambda i: (0, 0)),
                  pl.BlockSpec((Q_LORA, MLA_HEADS * LANES), lambda i: (0, 0))],
        out_specs=[pl.BlockSpec((MLA_HEADS, tm, LANES), lambda i: (0, i, 0)),
                   pl.BlockSpec((tm, KV_LORA), lambda i: (i, 0)),
                   pl.BlockSpec((tm, LANES), lambda i: (i, 0))],
        compiler_params=_cparams(1),
        name="mla_prep",
    )(misc, cos, sin, wts["q_norm"], wts["kv_norm"], wts["w_q1"], wts["w_q2"])


def _kv_body(ckv_ref, kr_ref, wk_ref, wv_ref, one_ref, kt_ref, v_ref):
    c = ckv_ref[...].astype(BF16)
    k = _dot(c, wk_ref[...])
    v = _dot(c, wv_ref[...])
    kr = kr_ref[...]
    one = one_ref[...]
    for h in range(MLA_HEADS):
        hs = slice(h * LANES, (h + 1) * LANES)
        kt_ref[h] = (k[:, hs] + kr).T.astype(BF16)
        v_ref[h] = (v[:, hs] + one).astype(BF16)


def _kv_expand(ckv, kr128, wts, *, tm):
    m = ckv.shape[0]
    assert m % tm == 0
    hw = MLA_HEADS * LANES
    return pl.pallas_call(
        _kv_body,
        out_shape=[jax.ShapeDtypeStruct((MLA_HEADS, LANES, m), BF16),
                   jax.ShapeDtypeStruct((MLA_HEADS, m, LANES), BF16)],
        grid=(m // tm,),
        in_specs=[pl.BlockSpec((tm, KV_LORA), lambda i: (i, 0)),
                  pl.BlockSpec((tm, LANES), lambda i: (i, 0)),
                  pl.BlockSpec((KV_LORA, hw), lambda i: (0, 0)),
                  pl.BlockSpec((KV_LORA, hw), lambda i: (0, 0)),
                  pl.BlockSpec((1, LANES), lambda i: (0, 0))],
        out_specs=[pl.BlockSpec((MLA_HEADS, LANES, tm), lambda i: (0, 0, i)),
                   pl.BlockSpec((MLA_HEADS, tm, LANES), lambda i: (0, i, 0))],
        compiler_params=_cparams(1),
        name="kv_expand",
    )(ckv, kr128, wts["w_k"], wts["w_v"], wts["one_col"])


def _attn_body(q_ref, kt_ref, v_ref, kpt_ref, vp_ref, o_ref, *scr, causal, prefix_valid, kv_valid):
    acc_scrs, m_scrs = scr[:HEAD_UNROLL], scr[HEAD_UNROLL:]
    i = pl.program_id(1)
    j = pl.program_id(2)
    tq = q_ref.shape[1]
    tk = kt_ref.shape[2]
    tp = kpt_ref.shape[2]
    n_trips = MLA_HEADS // HEAD_UNROLL

    def step(ktr, vr, mask, row0=0, k0=0, w=None):
        w = ktr.shape[2] - k0 if w is None else w
        rows = pl.ds(row0, tq - row0)
        keys = pl.ds(k0, w)

        def body(hp, carry):
            for u in range(HEAD_UNROLL):
                h = hp * HEAD_UNROLL + u
                acc_scr, m_scr = acc_scrs[u], m_scrs[u]
                s = _dot(q_ref[h, rows, :], ktr[h, :, keys])
                if mask is not None:
                    s = jnp.where(mask, s, NEG_BIG)
                m_prev = m_scr[hp, rows, :]
                m_new = jnp.maximum(m_prev, jnp.max(s, axis=1, keepdims=True))
                alpha = jnp.exp2(m_prev - m_new)
                p = jnp.exp2(s - pltpu.repeat(m_new, w // LANES, 1)).astype(BF16)
                acc_scr[hp, rows, :] = acc_scr[hp, rows, :] * alpha + _dot(p, vr[h, keys, :])
                m_scr[hp, rows, :] = m_new
            return carry

        lax.fori_loop(0, n_trips, body, 0)

    def finalize():
        lane = lax.broadcasted_iota(jnp.int32, (tq, LANES), 1)

        def pair(u, hp):
            a = acc_scrs[u][hp]
            b = acc_scrs[u + 1][hp]
            oa = a / a[:, V_HEAD_DIM:V_HEAD_DIM + 1]
            ob = b / b[:, V_HEAD_DIM:V_HEAD_DIM + 1]
            return jnp.where(lane < V_HEAD_DIM, oa, pltpu.roll(ob, V_HEAD_DIM, 1))

        def body(hp, carry):
            for u in range(0, HEAD_UNROLL, HEAD_PACK):
                slab = jnp.concatenate([pair(u + 2 * k, hp) for k in range(HEAD_PACK // 2)], axis=1)
                o_ref[hp * (HEAD_UNROLL // HEAD_PACK) + u // HEAD_PACK] = slab.astype(o_ref.dtype)
            return carry

        lax.fori_loop(0, n_trips, body, 0)

    @pl.when(j == 0)
    def _():
        for u in range(HEAD_UNROLL):
            acc_scrs[u][...] = jnp.zeros_like(acc_scrs[u])
            m_scrs[u][...] = jnp.full_like(m_scrs[u], NEG_BIG)
        step(kpt_ref, vp_ref, lax.broadcasted_iota(jnp.int32, (tq, tp), 1) < prefix_valid)

    if causal:
        r = tq // tk
        @pl.when(j < r * i)
        def _():
            step(kt_ref, v_ref, None)

        sub = min(tk, ATTN_DIAG_TK)
        for jj in range(r):
            @pl.when(j == r * i + jj)
            def _(jj=jj):
                for k0 in range(0, tk, sub):
                    row0 = jj * tk + k0
                    qc = (i * tq + row0 + lax.broadcasted_iota(jnp.int32, (tq - row0, sub), 0)) // CHUNK
                    kc = (j * tk + k0 + lax.broadcasted_iota(jnp.int32, (tq - row0, sub), 1)) // CHUNK
                    step(kt_ref, v_ref, kc <= qc, row0, k0, sub)

        @pl.when(j == r * (i + 1) - 1)
        def _():
            finalize()
    else:
        mask = None
        if kv_valid is not None:
            mask = (j * tk + lax.broadcasted_iota(jnp.int32, (tq, tk), 1)) < kv_valid
        step(kt_ref, v_ref, mask)

        @pl.when(j == pl.num_programs(2) - 1)
        def _():
            finalize()


def _attention(q, kt, v, kpt, vp, *, causal, tq, tk, prefix_valid, kv_valid=None):
    hh, b, t, _ = q.shape
    tkv = v.shape[2]
    assert t % tq == 0 and tkv % tk == 0
    nkb = tkv // tk
    if causal:
        assert tq % tk == 0 and t == tkv and tk % CHUNK == 0
        r = tq // tk
        kblk = lambda i, j: jnp.minimum(r * (i + 1) - 1, j)
    else:
        kblk = lambda i, j: j
    tp = vp.shape[1]
    return pl.pallas_call(
        functools.partial(_attn_body, causal=causal, prefix_valid=prefix_valid, kv_valid=kv_valid),
        out_shape=jax.ShapeDtypeStruct((HEAD_SLABS, b, t, SLAB_W), BF16),
        grid=(b, t // tq, nkb),
        in_specs=[pl.BlockSpec((hh, None, tq, LANES), lambda bi, i, j: (0, bi, i, 0)),
                  pl.BlockSpec((hh, LANES, tk), lambda bi, i, j: (0, 0, bi * nkb + kblk(i, j))),
                  pl.BlockSpec((hh, None, tk, LANES), lambda bi, i, j: (0, bi, kblk(i, j), 0)),
                  pl.BlockSpec((hh, LANES, tp), lambda bi, i, j: (0, 0, 0)),
                  pl.BlockSpec((hh, tp, LANES), lambda bi, i, j: (0, 0, 0))],
        out_specs=pl.BlockSpec((HEAD_SLABS, None, tq, SLAB_W), lambda bi, i, j: (0, bi, i, 0)),
        scratch_shapes=[pltpu.VMEM((hh // HEAD_UNROLL, tq, LANES), F32)] * (2 * HEAD_UNROLL),
        compiler_params=_cparams(3),
        name="attention",
    )(q, kt, v, kpt, vp)


def _mix_body(h_ref, y_ref, o_ref, gate_ref, ws_ref, wm_ref, wx_ref, out_ref):
    y_s = _dot(y_ref[...], ws_ref[...])
    y_m = _dot(o_ref[0], wm_ref[0])
    for hp in range(1, HEAD_SLABS):
        y_m = y_m + _dot(o_ref[hp], wm_ref[hp])
    merged = gate_ref[:, :D_MODEL].astype(F32) * y_s + gate_ref[:, D_MODEL:].astype(F32) * y_m
    out_ref[...] = h_ref[...] + _dot(merged.astype(BF16), wx_ref[...])


def _mix(h, y, o, gate, wts, *, tm_pref=256):
    m = h.shape[0]
    tm = _row_tile(m, tm_pref)
    return pl.pallas_call(
        _mix_body,
        out_shape=jax.ShapeDtypeStruct((m, D_MODEL), F32),
        grid=(m // tm,),
        in_specs=[pl.BlockSpec((tm, D_MODEL), lambda i: (i, 0)),
                  pl.BlockSpec((tm, D_INNER), lambda i: (i, 0)),
                  pl.BlockSpec((HEAD_SLABS, tm, SLAB_W), lambda i: (0, i, 0)),
                  pl.BlockSpec((tm, 2 * D_MODEL), lambda i: (i, 0)),
                  pl.BlockSpec((D_INNER, D_MODEL), lambda i: (0, 0)),
                  pl.BlockSpec((HEAD_SLABS, SLAB_W, D_MODEL), lambda i: (0, 0, 0)),
                  pl.BlockSpec((D_MODEL, D_MODEL), lambda i: (0, 0))],
        out_specs=pl.BlockSpec((tm, D_MODEL), lambda i: (i, 0)),
        compiler_params=_cparams(1),
        name="mix",
    )(h, y, o, gate, wts["w_ssm_out"], wts["w_mla_out"], wts["w_mix_out"])


def _prep_weights(p):
    w_in = p["w_in"]
    w_kr = w_in[:, OFF_KROPE:OFF_GATE]
    half = QK_ROPE_DIM // 2
    w_kr_sw = jnp.concatenate([w_kr[:, half:], w_kr[:, :half]], axis=1)
    zc = lambda n: jnp.zeros((D_MODEL, n), F32)
    w_misc = jnp.concatenate([
        w_in[:, OFF_CQ:OFF_CKV], w_in[:, OFF_CKV:OFF_KROPE],
        zc(QK_NOPE_DIM), w_kr, zc(LANES - QK_NOPE_DIM - QK_ROPE_DIM),
        zc(QK_NOPE_DIM), w_kr_sw, zc(LANES - QK_NOPE_DIM - QK_ROPE_DIM),
        w_in[:, OFF_DT:OFF_CQ], zc(LANES - SSM_HEADS)], axis=1)

    wq = p["w_uq"].reshape(Q_LORA, MLA_HEADS, QK_NOPE_DIM + QK_ROPE_DIM)
    q_nope, q_rope = wq[..., :QK_NOPE_DIM], wq[..., QK_NOPE_DIM:]
    q_rope_sw = jnp.concatenate([q_rope[..., half:], q_rope[..., :half]], axis=-1)
    zq = lambda n: jnp.zeros((Q_LORA, MLA_HEADS, n), F32)
    pad = LANES - QK_NOPE_DIM - QK_ROPE_DIM
    w_q1 = jnp.concatenate([q_nope, q_rope, zq(pad)], axis=-1).reshape(Q_LORA, MLA_HEADS * LANES)
    w_q2 = jnp.concatenate([zq(QK_NOPE_DIM), q_rope_sw, zq(pad)], axis=-1).reshape(Q_LORA, MLA_HEADS * LANES)

    wkv = p["w_ukv"].reshape(KV_LORA, MLA_HEADS, QK_NOPE_DIM + V_HEAD_DIM)
    zk = jnp.zeros((KV_LORA, MLA_HEADS, LANES - QK_NOPE_DIM), F32)
    w_k = jnp.concatenate([wkv[..., :QK_NOPE_DIM], zk], axis=-1).reshape(KV_LORA, MLA_HEADS * LANES)
    zv = jnp.zeros((KV_LORA, MLA_HEADS, LANES - V_HEAD_DIM), F32)
    w_v = jnp.concatenate([wkv[..., QK_NOPE_DIM:], zv], axis=-1).reshape(KV_LORA, MLA_HEADS * LANES)

    w_mla_out = p["mla_w_out"].reshape(HEAD_SLABS, SLAB_W, D_MODEL)

    expand = jnp.repeat(jnp.eye(SSM_HEADS, dtype=F32), SSM_HEAD_DIM, axis=1)
    one_col = (jnp.arange(LANES) == V_HEAD_DIM).astype(F32).reshape(1, LANES)
    return {
        "w_z": w_in[:, OFF_Z:OFF_XBC].astype(BF16),
        "w_xbc": w_in[:, OFF_XBC:OFF_DT].astype(BF16),
        "w_gate": w_in[:, OFF_GATE:].astype(BF16),
        "w_misc": w_misc.astype(BF16),
        "w_q1": w_q1.astype(BF16), "w_q2": w_q2.astype(BF16),
        "w_k": w_k.astype(BF16), "w_v": w_v.astype(BF16),
        "w_mla_out": w_mla_out.astype(BF16),
        "w_ssm_out": p["ssm_w_out"].astype(BF16),
        "w_mix_out": p["w_mix_out"].astype(BF16),
        "ffn1_w_gu": p["ffn1_w_gu"].astype(BF16), "ffn1_w_down": p["ffn1_w_down"].astype(BF16),
        "ffn2_w_gu": p["ffn2_w_gu"].astype(BF16), "ffn2_w_down": p["ffn2_w_down"].astype(BF16),
        "gate_bias": p["gate_bias"].reshape(1, -1),
        "conv_w": p["conv_w"], "conv_b": p["conv_b"].reshape(1, -1),
        "dt_bias": p["dt_bias"].reshape(1, -1), "a_log": p["a_log"].reshape(1, -1),
        "dvec": jnp.repeat(p["ssm_d"], SSM_HEAD_DIM).reshape(1, -1),
        "ssm_norm": p["ssm_norm"].reshape(1, -1),
        "expand": expand.astype(BF16), "one_col": one_col,
        "q_norm": p["q_norm"].reshape(1, -1), "kv_norm": p["kv_norm"].reshape(1, -1),
    }


def _rope_tables(pos):
    half = QK_ROPE_DIM // 2
    inv = ROPE_THETA ** (-jnp.arange(half, dtype=F32) / half)
    ang = pos.astype(F32)[:, None] * inv[None, :]
    c, s = jnp.cos(ang), jnp.sin(ang)
    t = pos.shape[0]
    pad = jnp.zeros((t, LANES - QK_NOPE_DIM - QK_ROPE_DIM), F32)
    cos = jnp.concatenate([jnp.ones((t, QK_NOPE_DIM), F32), c, c, pad], axis=1)
    sin = jnp.concatenate([jnp.zeros((t, QK_NOPE_DIM), F32), -s, s, pad], axis=1)
    return cos, sin


def _pad_rows(a, axis, to):
    padw = [(0, 0)] * a.ndim
    padw[axis] = (0, to - a.shape[axis])
    return jnp.pad(a, padw)


PROJ_CW = 512


def _in_proj_body(u_ref, wz_ref, wx_ref, wg_ref, wm_ref, gb_ref, zs_ref, xbc_ref, gate_ref, misc_ref):
    x = u_ref[...]
    for c in range(0, D_INNER, PROJ_CW):
        cs = slice(c, c + PROJ_CW)
        z = _dot(x, wz_ref[:, cs])
        zs_ref[:, cs] = (z * jax.nn.sigmoid(z)).astype(zs_ref.dtype)
    for c in range(0, CONV_DIM, PROJ_CW):
        cs = slice(c, c + PROJ_CW)
        xbc_ref[:, cs] = _dot(x, wx_ref[:, cs])
    for c in range(0, 2 * D_MODEL, PROJ_CW):
        cs = slice(c, c + PROJ_CW)
        gate_ref[:, cs] = jax.nn.sigmoid(_dot(x, wg_ref[:, cs]) + gb_ref[:, cs]).astype(gate_ref.dtype)
    for c in range(0, MISC_W, PROJ_CW):
        cs = slice(c, c + PROJ_CW)
        misc_ref[:, cs] = _dot(x, wm_ref[:, cs])


def _in_proj(u, wts, *, tm_pref=512):
    m = u.shape[0]
    tm = _row_tile(m, tm_pref)
    widths = (D_INNER, CONV_DIM, 2 * D_MODEL, MISC_W)
    dtypes = (BF16, F32, BF16, F32)
    resident = lambda n: pl.BlockSpec((D_MODEL, n), lambda i: (0, 0), pipeline_mode=pl.Buffered(1))
    return pl.pallas_call(
        _in_proj_body,
        out_shape=[jax.ShapeDtypeStruct((m, n), dt) for n, dt in zip(widths, dtypes)],
        grid=(m // tm,),
        in_specs=[pl.BlockSpec((tm, D_MODEL), lambda i: (i, 0))] + [resident(n) for n in widths]
                 + [pl.BlockSpec((1, 2 * D_MODEL), lambda i: (0, 0))],
        out_specs=[pl.BlockSpec((tm, n), lambda i: (i, 0)) for n in widths],
        compiler_params=_cparams(1),
        name="in_proj",
    )(u, wts["w_z"], wts["w_xbc"], wts["w_gate"], wts["w_misc"], wts["gate_bias"])


def kernel(x_prompt, x_sample, cache_ckv, cache_krope, state_ssm, state_conv, meta_tokens, ffn1_norm, ffn1_w_gu, ffn1_w_down, mix_norm, w_in, gate_bias, conv_w, conv_b, dt_bias, a_log, ssm_d, ssm_norm, ssm_w_out, q_norm, w_uq, kv_norm, w_ukv, mla_w_out, w_mix_out, ffn2_norm, ffn2_w_gu, ffn2_w_down, final_norm):
    b, seq, _ = x_prompt.shape
    db, dseq, _ = x_sample.shape
    past = cache_ckv.shape[2]
    n_meta = meta_tokens.shape[0]
    nd = db * dseq
    wts = _prep_weights({
        "w_in": w_in[0], "w_uq": w_uq[0], "w_ukv": w_ukv[0], "mla_w_out": mla_w_out[0],
        "ssm_w_out": ssm_w_out[0], "w_mix_out": w_mix_out[0],
        "ffn1_w_gu": ffn1_w_gu[0], "ffn1_w_down": ffn1_w_down[0],
        "ffn2_w_gu": ffn2_w_gu[0], "ffn2_w_down": ffn2_w_down[0],
        "gate_bias": gate_bias[0], "conv_w": conv_w[0], "conv_b": conv_b[0], "dt_bias": dt_bias[0],
        "a_log": a_log[0], "ssm_d": ssm_d[0], "ssm_norm": ssm_norm[0],
        "q_norm": q_norm[0], "kv_norm": kv_norm[0]})

    xb = x_prompt.reshape(b * seq, D_MODEL)
    xs = jnp.concatenate([x_sample.reshape(nd, D_MODEL), meta_tokens.astype(F32)], axis=0)

    hb, ub = _ffn(xb, ffn1_norm[0], wts["ffn1_w_gu"], wts["ffn1_w_down"], mix_norm[0], emit_h=True, out_dtype=BF16)
    hs, us = _ffn(xs, ffn1_norm[0], wts["ffn1_w_gu"], wts["ffn1_w_down"], mix_norm[0], emit_h=True, out_dtype=BF16)
    zb, xbcb, gateb, miscb = _in_proj(ub, wts)
    zs, xbcs, gates, miscs = _in_proj(us, wts)

    tm_len, td_len = -(-n_meta // 8) * 8, -(-dseq // 8) * 8
    meta3 = lambda a: _pad_rows(a[nd:][None], 1, tm_len)
    _, ssm_meta = _ssd(meta3(xbcs), meta3(miscs), meta3(zs),
                       jnp.zeros((1, CONV_WIDTH - 1, CONV_DIM), F32),
                       jnp.zeros((1, SSM_HEADS, SSM_HEAD_DIM, SSM_STATE), F32), wts,
                       t_valid=None if tm_len == n_meta else n_meta)
    conv_meta = xbcs[nd + n_meta - (CONV_WIDTH - 1):nd + n_meta][None]
    yb, ssm_p = _ssd(xbcb.reshape(b, seq, -1), miscb.reshape(b, seq, -1), zb.reshape(b, seq, -1),
                     jnp.broadcast_to(conv_meta, (b,) + conv_meta.shape[1:]),
                     jnp.broadcast_to(ssm_meta, (b,) + ssm_meta.shape[1:]), wts, t_valid=None)
    dec3 = lambda a: _pad_rows(a[:nd].reshape(db, dseq, -1), 1, td_len)
    yd, ssm_s = _ssd(dec3(xbcs), dec3(miscs), dec3(zs), state_conv[0], state_ssm[0], wts,
                     t_valid=None if td_len == dseq else dseq)
    yd = yd[:, :dseq].reshape(nd, D_INNER)
    conv_p = xbcb.reshape(b, seq, -1)[:, seq - (CONV_WIDTH - 1):]
    xbcd = xbcs[:nd].reshape(db, dseq, -1)
    conv_s = jnp.concatenate([state_conv[0].astype(F32), xbcd], axis=1)[:, -(CONV_WIDTH - 1):]

    tq = _row_tile(seq, 512)
    cos_b, sin_b = _rope_tables(jnp.arange(seq, dtype=jnp.int32))
    pos_s = jnp.concatenate([jnp.tile(past + jnp.arange(dseq, dtype=jnp.int32), db),
                             jnp.arange(n_meta, dtype=jnp.int32) - n_meta])
    cos_s, sin_s = _rope_tables(pos_s)
    qb, ckvb, krb = _mla_prep(miscb, cos_b, sin_b, seq // tq, wts, tm=tq)
    qs, ckvs, krs = _mla_prep(miscs, cos_s, sin_s, 1, wts, tm=miscs.shape[0])
    kb, vb = _kv_expand(ckvb, krb, wts, tm=tq)
    kp, vp = _kv_expand(_pad_rows(ckvs[nd:], 0, LANES), _pad_rows(krs[nd:], 0, LANES), wts, tm=LANES)
    tkd = -(-(past + dseq) // LANES) * LANES
    ckv_d = _pad_rows(jnp.concatenate([cache_ckv[0].astype(F32), ckvs[:nd].reshape(db, dseq, -1)], axis=1), 1, tkd)
    kr_cache = jnp.pad(cache_krope[0].astype(F32),
                       ((0, 0), (0, 0), (QK_NOPE_DIM, LANES - QK_NOPE_DIM - QK_ROPE_DIM)))
    kr_d = _pad_rows(jnp.concatenate([kr_cache, krs[:nd].reshape(db, dseq, -1)], axis=1), 1, tkd)
    kd, vd = _kv_expand(ckv_d.reshape(db * tkd, -1), kr_d.reshape(db * tkd, -1), wts,
                        tm=_row_tile(db * tkd, 512))

    hsplit = lambda a, nb: a.reshape(MLA_HEADS, nb, -1, LANES)
    ob = _attention(hsplit(qb, b), kb, hsplit(vb, b), kp, vp,
                    causal=True, tq=_row_tile(seq, ATTN_TQ), tk=_row_tile(seq, ATTN_TK), prefix_valid=n_meta)
    od = _attention(hsplit(qs[:, :nd], db), kd, hsplit(vd, db), kp, vp,
                    causal=False, tq=dseq, tk=tkd, prefix_valid=n_meta, kv_valid=past + dseq)

    h2b = _mix(hb, yb.reshape(b * seq, -1), ob.reshape(HEAD_SLABS, b * seq, SLAB_W), gateb, wts)
    h2d = _mix(hs[:nd], yd, od.reshape(HEAD_SLABS, nd, SLAB_W), gates[:nd], wts)
    (y_p,) = _ffn(h2b, ffn2_norm[0], wts["ffn2_w_gu"], wts["ffn2_w_down"], final_norm, emit_h=False, out_dtype=F32)
    (y_s,) = _ffn(h2d, ffn2_norm[0], wts["ffn2_w_gu"], wts["ffn2_w_down"], final_norm, emit_h=False, out_dtype=F32)

    kr32 = lambda a: a[..., QK_NOPE_DIM:QK_NOPE_DIM + QK_ROPE_DIM]
    ckv_meta = jnp.broadcast_to(ckvs[nd:][None], (b, n_meta, KV_LORA))
    kr_meta = jnp.broadcast_to(kr32(krs[nd:])[None], (b, n_meta, QK_ROPE_DIM))
    ckv_p = jnp.concatenate([ckv_meta, ckvb.reshape(b, seq, -1)], axis=1)
    kr_p = jnp.concatenate([kr_meta, kr32(krb).reshape(b, seq, -1)], axis=1)
    return (y_p.reshape(b, seq, D_MODEL), y_s.reshape(db, dseq, D_MODEL),
            ckv_p[None], kr_p[None], ssm_p[None], conv_p[None],
            ckvs[:nd].reshape(db, dseq, -1)[None], kr32(krs[:nd]).reshape(db, dseq, -1)[None],
            ssm_s[None], conv_s[None])
```

```python
import functools
import math

import jax
import jax.numpy as jnp
from jax import lax
from jax.experimental import pallas as pl
from jax.experimental.pallas import tpu as pltpu

F32 = jnp.float32
BF16 = jnp.bfloat16

D_MODEL = 1024
CHUNK = 64
FFN_DIM = 2816
NORM_EPS = 1e-6
D_INNER = 2048
SSM_HEAD_DIM = 64
SSM_HEADS = 32
SSM_GROUPS = 4
SSM_STATE = 128
CONV_WIDTH = 4
CONV_DIM = D_INNER + 2 * SSM_GROUPS * SSM_STATE
MLA_HEADS = 16
HEAD_PACK = 4
HEAD_SLABS = MLA_HEADS // HEAD_PACK
Q_LORA = 384
KV_LORA = 256
QK_NOPE_DIM = 64
QK_ROPE_DIM = 32
V_HEAD_DIM = 64
SLAB_W = HEAD_PACK * V_HEAD_DIM
ROPE_THETA = 10000.0
OFF_Z = 0
OFF_XBC = OFF_Z + D_INNER
OFF_DT = OFF_XBC + CONV_DIM
OFF_CQ = OFF_DT + SSM_HEADS
OFF_CKV = OFF_CQ + Q_LORA
OFF_KROPE = OFF_CKV + KV_LORA
OFF_GATE = OFF_KROPE + QK_ROPE_DIM

LANES = 128
SSD_L = 128
FFN_TF = 256
HEAD_UNROLL = 8
ATTN_TQ = 1024
ATTN_TK = 1024
ATTN_DIAG_TK = 256
NEG_BIG = -1e30
VMEM_LIMIT = 52 * 1024 * 1024

MISC_CQ = 0
MISC_CKV = 384
MISC_KRA = 640
MISC_KRB = 768
MISC_DT = 896
MISC_W = 1024


def _cparams(n_grid):
    return pltpu.CompilerParams(dimension_semantics=("arbitrary",) * n_grid,
                                vmem_limit_bytes=VMEM_LIMIT)


def _row_tile(m, pref):
    if m <= pref:
        return m
    t = pref
    while t >= LANES:
        if m % t == 0:
            return t
        t //= 2
    return m


def _rms(x, g):
    return x * lax.rsqrt(jnp.mean(x * x, axis=-1, keepdims=True) + NORM_EPS) * g


def _dot(a, b):
    return jnp.dot(a, b, preferred_element_type=F32)


def _ffn_body(x_ref, g1_ref, wgu_ref, wd_ref, g2_ref, *outs, emit_h):
    x = x_ref[...]
    xn = _rms(x, g1_ref[...]).astype(BF16)
    acc = None
    for f in range(0, FFN_DIM, FFN_TF):
        g = _dot(xn, wgu_ref[:, f:f + FFN_TF])
        u = _dot(xn, wgu_ref[:, FFN_DIM + f:FFN_DIM + f + FFN_TF])
        act = (g * jax.nn.sigmoid(g) * u).astype(BF16)
        d = _dot(act, wd_ref[f:f + FFN_TF, :])
        acc = d if acc is None else acc + d
    h = x + 0.5 * acc
    if emit_h:
        outs[0][...] = h
    outs[-1][...] = _rms(h, g2_ref[...]).astype(outs[-1].dtype)


def _ffn(x, g1, w_gu, w_down, g2, *, emit_h, out_dtype, tm_pref=512):
    m = x.shape[0]
    tm = _row_tile(m, tm_pref)
    row = pl.BlockSpec((tm, D_MODEL), lambda i: (i, 0))
    vec = pl.BlockSpec((1, D_MODEL), lambda i: (0, 0))
    out_shape = [jax.ShapeDtypeStruct((m, D_MODEL), out_dtype)]
    out_specs = [row]
    if emit_h:
        out_shape = [jax.ShapeDtypeStruct((m, D_MODEL), F32)] + out_shape
        out_specs = [row, row]
    return pl.pallas_call(
        functools.partial(_ffn_body, emit_h=emit_h),
        out_shape=out_shape,
        grid=(m // tm,),
        in_specs=[row, vec,
                  pl.BlockSpec((D_MODEL, 2 * FFN_DIM), lambda i: (0, 0), pipeline_mode=pl.Buffered(1)),
                  pl.BlockSpec((FFN_DIM, D_MODEL), lambda i: (0, 0), pipeline_mode=pl.Buffered(1)),
                  vec],
        out_specs=out_specs,
        compiler_params=_cparams(1),
        name="ffn",
    )(x, g1.reshape(1, -1), w_gu, w_down, g2.reshape(1, -1))


def _softplus(x):
    return jnp.maximum(x, 0.0) + jnp.log1p(jnp.exp(-jnp.abs(x)))


def _expand_heads(x, e):
    hi = x.astype(BF16)
    lo = (x - hi.astype(F32)).astype(BF16)
    return _dot(hi, e) + _dot(lo, e)


def _ssd_body(xbc_ref, dtb_ref, zs_ref, convp_ref, ssmp_ref, cw_ref, cb_ref, dtbias_ref, alog_ref,
              dvec_ref, nrm_ref, e_ref, y_ref, ssm_ref, xpad, state, y_scr, *, t_valid):
    t = pl.program_id(1)
    L = SSD_L
    gs = SSM_STATE
    hg = SSM_HEADS // SSM_GROUPS
    gw = hg * SSM_HEAD_DIM

    @pl.when(t == 0)
    def _():
        xpad[0:8, :] = convp_ref[...]
        state[...] = ssmp_ref[...].T

    xpad[8:8 + L, :] = xbc_ref[...]

    def conv_act(c0, width):
        cs = slice(c0, c0 + width)
        acc = cb_ref[:, cs] + xpad[5:5 + L, cs] * cw_ref[0:1, cs]
        for k in range(1, CONV_WIDTH):
            acc = acc + xpad[5 + k:5 + k + L, cs] * cw_ref[k:k + 1, cs]
        return acc * jax.nn.sigmoid(acc)

    dt =_softplus(dtb_ref[:, 0:SSM_HEADS] + dtbias_ref[...])
    if t_valid is not None:
        rows = t * L + lax.broadcasted_iota(jnp.int32, (L, SSM_HEADS), 0)
        dt = jnp.where(rows < t_valid, dt, 0.0)
    da = dt * (-jnp.exp(alog_ref[...]))
    ri = lax.broadcasted_iota(jnp.int32, (L, L), 0)
    ci = lax.broadcasted_iota(jnp.int32, (L, L), 1)
    causal = ri >= ci
    a_cs = jnp.dot(causal.astype(F32), da, precision=lax.Precision.HIGHEST,
                   preferred_element_type=F32)
    eye = (lax.broadcasted_iota(jnp.int32, (SSM_HEADS, SSM_HEADS), 0)
           == lax.broadcasted_iota(jnp.int32, (SSM_HEADS, SSM_HEADS), 1)).astype(F32)
    a_cs_t = lax.dot_general(eye, a_cs, (((1,), (1,)), ((), ())), precision=lax.Precision.HIGHEST,
                             preferred_element_type=F32)
    total = a_cs[L - 1:L, :]

    exp_cs = jnp.exp(a_cs)
    w_in = jnp.exp(total - a_cs) * dt
    exp_tot = jnp.broadcast_to(jnp.exp(total), (8, SSM_HEADS))

    for g in range(SSM_GROUPS):
        gsl = slice(g * gw, (g + 1) * gw)
        xs = conv_act(g * gw, gw)
        bg = conv_act(D_INNER + g * gs, gs).astype(BF16)
        cg = conv_act(D_INNER + SSM_GROUPS * gs + g * gs, gs).astype(BF16)
        e = e_ref[:, gsl]
        xdt = xs * _expand_heads(dt, e)
        wx = (xs * _expand_heads(w_in, e)).astype(BF16)
        cb = lax.dot_general(cg, bg, (((1,), (1,)), ((), ())), preferred_element_type=F32)
        cb = jnp.where(causal, cb, 0.0)
        for hh in range(hg):
            h = g * hg + hh
            seg = a_cs[:, h:h + 1] - a_cs_t[h:h + 1, :]
            mh = (cb * jnp.exp(jnp.minimum(seg, 0.0))).astype(BF16)
            cs = slice(hh * SSM_HEAD_DIM, (hh + 1) * SSM_HEAD_DIM)
            y_scr[:, cs] = _dot(mh, xdt[:, cs].astype(BF16))
        st = state[:, gsl]
        y = y_scr[...] + _dot(cg, st.astype(BF16)) * _expand_heads(exp_cs, e) + dvec_ref[:, gsl] * xs
        upd = lax.dot_general(bg, wx, (((0,), (0,)), ((), ())), preferred_element_type=F32)
        state[:, gsl] = st * _expand_heads(exp_tot, e)[0:1, :] + upd
        y = y * zs_ref[:, gsl].astype(F32)
        y = y * lax.rsqrt(jnp.mean(y * y, axis=-1, keepdims=True) + NORM_EPS)
        y_ref[:, gsl] = (y * nrm_ref[:, gsl]).astype(y_ref.dtype)

    xpad[0:8, :] = xpad[L:L + 8, :]

    @pl.when(t == pl.num_programs(1) - 1)
    def _():
        ssm_ref[...] = state[...].T


def _ssd(xbc, misc, z, conv_prev, ssm_prev, wts, *, t_valid):
    b, t, _ = xbc.shape
    assert t % SSD_L == 0
    convp = jnp.pad(conv_prev.astype(F32), ((0, 0), (8 - (CONV_WIDTH - 1), 0), (0, 0)))
    ssmp = ssm_prev.astype(F32).reshape(b, D_INNER, SSM_STATE)
    tok = lambda w: pl.BlockSpec((None, SSD_L, w), lambda bi, ti: (bi, ti, 0))
    full = lambda r, c: pl.BlockSpec((r, c), lambda bi, ti: (0, 0))
    y, ssm = pl.pallas_call(
        functools.partial(_ssd_body, t_valid=t_valid),
        out_shape=[jax.ShapeDtypeStruct((b, t, D_INNER), BF16),
                   jax.ShapeDtypeStruct((b, D_INNER, SSM_STATE), F32)],
        grid=(b, t // SSD_L),
        in_specs=[tok(CONV_DIM),
                  pl.BlockSpec((None, SSD_L, LANES), lambda bi, ti: (bi, ti, MISC_DT // LANES)),
                  tok(D_INNER),
                  pl.BlockSpec((None, 8, CONV_DIM), lambda bi, ti: (bi, 0, 0)),
                  pl.BlockSpec((None, D_INNER, SSM_STATE), lambda bi, ti: (bi, 0, 0)),
                  full(CONV_WIDTH, CONV_DIM), full(1, CONV_DIM), full(1, SSM_HEADS), full(1, SSM_HEADS),
                  full(1, D_INNER), full(1, D_INNER), full(SSM_HEADS, D_INNER)],
        out_specs=[tok(D_INNER),
                   pl.BlockSpec((None, D_INNER, SSM_STATE), lambda bi, ti: (bi, 0, 0))],
        scratch_shapes=[pltpu.VMEM((SSD_L + 8, CONV_DIM), F32),
                        pltpu.VMEM((SSM_STATE, D_INNER), F32),
                        pltpu.VMEM((SSD_L, D_INNER // SSM_GROUPS), F32)],
        compiler_params=_cparams(2),
        name="ssd",
    )(xbc, misc, z, convp, ssmp, wts["conv_w"], wts["conv_b"], wts["dt_bias"], wts["a_log"],
      wts["dvec"], wts["ssm_norm"], wts["expand"])
    return y, ssm.reshape(b, SSM_HEADS, SSM_HEAD_DIM, SSM_STATE)


def _mla_prep_body(misc_ref, cos_ref, sin_ref, qn_ref, kvn_ref, w1_ref, w2_ref, q_ref, ckv_ref, kr_ref):
    cos = cos_ref[...]
    sin = sin_ref[...]
    qn = _rms(misc_ref[:, MISC_CQ:MISC_CQ + Q_LORA], qn_ref[...]).astype(BF16)
    scale = (QK_NOPE_DIM + QK_ROPE_DIM) ** -0.5 * math.log2(math.e)
    for hp in range(0, MLA_HEADS, 2):
        ps = slice(hp * LANES, (hp + 2) * LANES)
        q1 = _dot(qn, w1_ref[:, ps])
        q2 = _dot(qn, w2_ref[:, ps])
        for k in range(2):
            hs = slice(k * LANES, (k + 1) * LANES)
            q_ref[hp + k] = ((q1[:, hs] * cos + q2[:, hs] * sin) * scale).astype(BF16)
    ckv_ref[...] = _rms(misc_ref[:, MISC_CKV:MISC_CKV + KV_LORA], kvn_ref[...])
    kr_ref[...] = misc_ref[:, MISC_KRA:MISC_KRA + LANES] * cos + misc_ref[:, MISC_KRB:MISC_KRB + LANES] * sin


def _mla_prep(misc, cos, sin, n_pos_blocks, wts, *, tm):
    m = misc.shape[0]
    assert m % tm == 0
    return pl.pallas_call(
        _mla_prep_body,
        out_shape=[jax.ShapeDtypeStruct((MLA_HEADS, m, LANES), BF16),
                   jax.ShapeDtypeStruct((m, KV_LORA), F32),
                   jax.ShapeDtypeStruct((m, LANES), F32)],
        grid=(m // tm,),
        in_specs=[pl.BlockSpec((tm, MISC_W), lambda i: (i, 0)),
                  pl.BlockSpec((tm, LANES), lambda i: (i % n_pos_blocks, 0)),
                  pl.BlockSpec((tm, LANES), lambda i: (i % n_pos_blocks, 0)),
                  pl.BlockSpec((1, Q_LORA), lambda i: (0, 0)),
                  pl.BlockSpec((1, KV_LORA), lambda i: (0, 0)),
                  pl.BlockSpec((Q_LORA, MLA_HEADS * LANES), lambda i: (0, 0)),
                  pl.BlockSpec((Q_LORA, MLA_HEADS * LANES), lambda i: (0, 0))],
        out_specs=[pl.BlockSpec((MLA_HEADS, tm, LANES), lambda i: (0, i, 0)),
                   pl.BlockSpec((tm, KV_LORA), lambda i: (i, 0)),
                   pl.BlockSpec((tm, LANES), lambda i: (i, 0))],
        compiler_params=_cparams(1),
        name="mla_prep",
    )(misc, cos, sin, wts["q_norm"], wts["kv_norm"], wts["w_q1"], wts["w_q2"])


def _kv_body(ckv_ref, kr_ref, wk_ref, wv_ref, one_ref, kt_ref, v_ref):
    c = ckv_ref[...].astype(BF16)
    k = _dot(c, wk_ref[...])
    v = _dot(c, wv_ref[...])
    kr = kr_ref[...]
    one = one_ref[...]
    for h in range(MLA_HEADS):
        hs = slice(h * LANES, (h + 1) * LANES)
        kt_ref[h] = (k[:, hs] + kr).T.astype(BF16)
        v_ref[h] = (v[:, hs] + one).astype(BF16)


def _kv_expand(ckv, kr128, wts, *, tm):
    m = ckv.shape[0]
    assert m % tm == 0
    hw = MLA_HEADS * LANES
    return pl.pallas_call(
        _kv_body,
        out_shape=[jax.ShapeDtypeStruct((MLA_HEADS, LANES, m), BF16),
                   jax.ShapeDtypeStruct((MLA_HEADS, m, LANES), BF16)],
        grid=(m // tm,),
        in_specs=[pl.BlockSpec((tm, KV_LORA), lambda i: (i, 0)),
                  pl.BlockSpec((tm, LANES), lambda i: (i, 0)),
                  pl.BlockSpec((KV_LORA, hw), lambda i: (0, 0)),
                  pl.BlockSpec((KV_LORA, hw), lambda i: (0, 0)),
                  pl.BlockSpec((1, LANES), lambda i: (0, 0))],
        out_specs=[pl.BlockSpec((MLA_HEADS, LANES, tm), lambda i: (0, 0, i)),
                   pl.BlockSpec((MLA_HEADS, tm, LANES), lambda i: (0, i, 0))],
        compiler_params=_cparams(1),
        name="kv_expand",
    )(ckv, kr128, wts["w_k"], wts["w_v"], wts["one_col"])


def _attn_body(q_ref, kt_ref, v_ref, kpt_ref, vp_ref, o_ref, *scr, causal, prefix_valid, kv_valid):
    acc_scrs, m_scrs = scr[:HEAD_UNROLL], scr[HEAD_UNROLL:]
    i = pl.program_id(1)
    j = pl.program_id(2)
    tq = q_ref.shape[1]
    tk = kt_ref.shape[2]
    tp = kpt_ref.shape[2]
    n_trips = MLA_HEADS // HEAD_UNROLL

    def step(ktr, vr, mask, row0=0, k0=0, w=None):
        w = ktr.shape[2] - k0 if w is None else w
        rows = pl.ds(row0, tq - row0)
        keys = pl.ds(k0, w)

        def body(hp, carry):
            for u in range(HEAD_UNROLL):
                h = hp * HEAD_UNROLL + u
                acc_scr, m_scr = acc_scrs[u], m_scrs[u]
                s = _dot(q_ref[h, rows, :], ktr[h, :, keys])
                if mask is not None:
                    s = jnp.where(mask, s, NEG_BIG)
                m_prev = m_scr[hp, rows, :]
                m_new = jnp.maximum(m_prev, jnp.max(s, axis=1, keepdims=True))
                alpha = jnp.exp2(m_prev - m_new)
                p = jnp.exp2(s - pltpu.repeat(m_new, w // LANES, 1)).astype(BF16)
                acc_scr[hp, rows, :] = acc_scr[hp, rows, :] * alpha + _dot(p, vr[h, keys, :])
                m_scr[hp, rows, :] = m_new
            return carry

        lax.fori_loop(0, n_trips, body, 0)

    def finalize():
        lane = lax.broadcasted_iota(jnp.int32, (tq, LANES), 1)

        def pair(u, hp):
            a = acc_scrs[u][hp]
            b = acc_scrs[u + 1][hp]
            oa = a / a[:, V_HEAD_DIM:V_HEAD_DIM + 1]
            ob = b / b[:, V_HEAD_DIM:V_HEAD_DIM + 1]
            return jnp.where(lane < V_HEAD_DIM, oa, pltpu.roll(ob, V_HEAD_DIM, 1))

        def body(hp, carry):
            for u in range(0, HEAD_UNROLL, HEAD_PACK):
                slab = jnp.concatenate([pair(u + 2 * k, hp) for k in range(HEAD_PACK // 2)], axis=1)
                o_ref[hp * (HEAD_UNROLL // HEAD_PACK) + u // HEAD_PACK] = slab.astype(o_ref.dtype)
            return carry

        lax.fori_loop(0, n_trips, body, 0)

    @pl.when(j == 0)
    def _():
        for u in range(HEAD_UNROLL):
            acc_scrs[u][...] = jnp.zeros_like(acc_scrs[u])
            m_scrs[u][...] = jnp.full_like(m_scrs[u], NEG_BIG)
        step(kpt_ref, vp_ref, lax.broadcasted_iota(jnp.int32, (tq, tp), 1) < prefix_valid)

    if causal:
        r = tq // tk
        @pl.when(j < r * i)
        def _():
            step(kt_ref, v_ref, None)

        sub = min(tk, ATTN_DIAG_TK)
        for jj in range(r):
            @pl.when(j == r * i + jj)
            def _(jj=jj):
                for k0 in range(0, tk, sub):
                    row0 = jj * tk + k0
                    qc = (i * tq + row0 + lax.broadcasted_iota(jnp.int32, (tq - row0, sub), 0)) // CHUNK
                    kc = (j * tk + k0 + lax.broadcasted_iota(jnp.int32, (tq - row0, sub), 1)) // CHUNK
                    step(kt_ref, v_ref, kc <= qc, row0, k0, sub)

        @pl.when(j == r * (i + 1) - 1)
        def _():
            finalize()
    else:
        mask = None
        if kv_valid is not None:
            mask = (j * tk + lax.broadcasted_iota(jnp.int32, (tq, tk), 1)) < kv_valid
        step(kt_ref, v_ref, mask)

        @pl.when(j == pl.num_programs(2) - 1)
        def _():
            finalize()


def _attention(q, kt, v, kpt, vp, *, causal, tq, tk, prefix_valid, kv_valid=None):
    hh, b, t, _ = q.shape
    tkv = v.shape[2]
    assert t % tq == 0 and tkv % tk == 0
    nkb = tkv // tk
    if causal:
        assert tq % tk == 0 and t == tkv and tk % CHUNK == 0
        r = tq // tk
        kblk = lambda i, j: jnp.minimum(r * (i + 1) - 1, j)
    else:
        kblk = lambda i, j: j
    tp = vp.shape[1]
    return pl.pallas_call(
        functools.partial(_attn_body, causal=causal, prefix_valid=prefix_valid, kv_valid=kv_valid),
        out_shape=jax.ShapeDtypeStruct((HEAD_SLABS, b, t, SLAB_W), BF16),
        grid=(b, t // tq, nkb),
        in_specs=[pl.BlockSpec((hh, None, tq, LANES), lambda bi, i, j: (0, bi, i, 0)),
                  pl.BlockSpec((hh, LANES, tk), lambda bi, i, j: (0, 0, bi * nkb + kblk(i, j))),
                  pl.BlockSpec((hh, None, tk, LANES), lambda bi, i, j: (0, bi, kblk(i, j), 0)),
                  pl.BlockSpec((hh, LANES, tp), lambda bi, i, j: (0, 0, 0)),
                  pl.BlockSpec((hh, tp, LANES), lambda bi, i, j: (0, 0, 0))],
        out_specs=pl.BlockSpec((HEAD_SLABS, None, tq, SLAB_W), lambda bi, i, j: (0, bi, i, 0)),
        scratch_shapes=[pltpu.VMEM((hh // HEAD_UNROLL, tq, LANES), F32)] * (2 * HEAD_UNROLL),
        compiler_params=_cparams(3),
        name="attention",
    )(q, kt, v, kpt, vp)


def _mix_body(h_ref, y_ref, o_ref, gate_ref, ws_ref, wm_ref, wx_ref, out_ref):
    y_s = _dot(y_ref[...], ws_ref[...])
    y_m = _dot(o_ref[0], wm_ref[0])
    for hp in range(1, HEAD_SLABS):
        y_m = y_m + _dot(o_ref[hp], wm_ref[hp])
    merged = gate_ref[:, :D_MODEL].astype(F32) * y_s + gate_ref[:, D_MODEL:].astype(F32) * y_m
    out_ref[...] = h_ref[...] + _dot(merged.astype(BF16), wx_ref[...])


def _mix(h, y, o, gate, wts, *, tm_pref=256):
    m = h.shape[0]
    tm = _row_tile(m, tm_pref)
    return pl.pallas_call(
        _mix_body,
        out_shape=jax.ShapeDtypeStruct((m, D_MODEL), F32),
        grid=(m // tm,),
        in_specs=[pl.BlockSpec((tm, D_MODEL), lambda i: (i, 0)),
                  pl.BlockSpec((tm, D_INNER), lambda i: (i, 0)),
                  pl.BlockSpec((HEAD_SLABS, tm, SLAB_W), lambda i: (0, i, 0)),
                  pl.BlockSpec((tm, 2 * D_MODEL), lambda i: (i, 0)),
                  pl.BlockSpec((D_INNER, D_MODEL), lambda i: (0, 0)),
                  pl.BlockSpec((HEAD_SLABS, SLAB_W, D_MODEL), lambda i: (0, 0, 0)),
                  pl.BlockSpec((D_MODEL, D_MODEL), lambda i: (0, 0))],
        out_specs=pl.BlockSpec((tm, D_MODEL), lambda i: (i, 0)),
        compiler_params=_cparams(1),
        name="mix",
    )(h, y, o, gate, wts["w_ssm_out"], wts["w_mla_out"], wts["w_mix_out"])


def _prep_weights(p):
    w_in = p["w_in"]
    w_kr = w_in[:, OFF_KROPE:OFF_GATE]
    half = QK_ROPE_DIM // 2
    w_kr_sw = jnp.concatenate([w_kr[:, half:], w_kr[:, :half]], axis=1)
    zc = lambda n: jnp.zeros((D_MODEL, n), F32)
    w_misc = jnp.concatenate([
        w_in[:, OFF_CQ:OFF_CKV], w_in[:, OFF_CKV:OFF_KROPE],
        zc(QK_NOPE_DIM), w_kr, zc(LANES - QK_NOPE_DIM - QK_ROPE_DIM),
        zc(QK_NOPE_DIM), w_kr_sw, zc(LANES - QK_NOPE_DIM - QK_ROPE_DIM),
        w_in[:, OFF_DT:OFF_CQ], zc(LANES - SSM_HEADS)], axis=1)

    wq = p["w_uq"].reshape(Q_LORA, MLA_HEADS, QK_NOPE_DIM + QK_ROPE_DIM)
    q_nope, q_rope = wq[..., :QK_NOPE_DIM], wq[..., QK_NOPE_DIM:]
    q_rope_sw = jnp.concatenate([q_rope[..., half:], q_rope[..., :half]], axis=-1)
    zq = lambda n: jnp.zeros((Q_LORA, MLA_HEADS, n), F32)
    pad = LANES - QK_NOPE_DIM - QK_ROPE_DIM
    w_q1 = jnp.concatenate([q_nope, q_rope, zq(pad)], axis=-1).reshape(Q_LORA, MLA_HEADS * LANES)
    w_q2 = jnp.concatenate([zq(QK_NOPE_DIM), q_rope_sw, zq(pad)], axis=-1).reshape(Q_LORA, MLA_HEADS * LANES)

    wkv = p["w_ukv"].reshape(KV_LORA, MLA_HEADS, QK_NOPE_DIM + V_HEAD_DIM)
    zk = jnp.zeros((KV_LORA, MLA_HEADS, LANES - QK_NOPE_DIM), F32)
    w_k = jnp.concatenate([wkv[..., :QK_NOPE_DIM], zk], axis=-1).reshape(KV_LORA, MLA_HEADS * LANES)
    zv = jnp.zeros((KV_LORA, MLA_HEADS, LANES - V_HEAD_DIM), F32)
    w_v = jnp.concatenate([wkv[..., QK_NOPE_DIM:], zv], axis=-1).reshape(KV_LORA, MLA_HEADS * LANES)

    w_mla_out = p["mla_w_out"].reshape(HEAD_SLABS, SLAB_W, D_MODEL)

    expand = jnp.repeat(jnp.eye(SSM_HEADS, dtype=F32), SSM_HEAD_DIM, axis=1)
    one_col = (jnp.arange(LANES) == V_HEAD_DIM).astype(F32).reshape(1, LANES)
    return {
        "w_z": w_in[:, OFF_Z:OFF_XBC].astype(BF16),
        "w_xbc": w_in[:, OFF_XBC:OFF_DT].astype(BF16),
        "w_gate": w_in[:, OFF_GATE:].astype(BF16),
        "w_misc": w_misc.astype(BF16),
        "w_q1": w_q1.astype(BF16), "w_q2": w_q2.astype(BF16),
        "w_k": w_k.astype(BF16), "w_v": w_v.astype(BF16),
        "w_mla_out": w_mla_out.astype(BF16),
        "w_ssm_out": p["ssm_w_out"].astype(BF16),
        "w_mix_out": p["w_mix_out"].astype(BF16),
        "ffn1_w_gu": p["ffn1_w_gu"].astype(BF16), "ffn1_w_down": p["ffn1_w_down"].astype(BF16),
        "ffn2_w_gu": p["ffn2_w_gu"].astype(BF16), "ffn2_w_down": p["ffn2_w_down"].astype(BF16),
        "gate_bias": p["gate_bias"].reshape(1, -1),
        "conv_w": p["conv_w"], "conv_b": p["conv_b"].reshape(1, -1),
        "dt_bias": p["dt_bias"].reshape(1, -1), "a_log": p["a_log"].reshape(1, -1),
        "dvec": jnp.repeat(p["ssm_d"], SSM_HEAD_DIM).reshape(1, -1),
        "ssm_norm": p["ssm_norm"].reshape(1, -1),
        "expand": expand.astype(BF16), "one_col": one_col,
        "q_norm": p["q_norm"].reshape(1, -1), "kv_norm": p["kv_norm"].reshape(1, -1),
    }


def _rope_tables(pos):
    half = QK_ROPE_DIM // 2
    inv = ROPE_THETA ** (-jnp.arange(half, dtype=F32) / half)
    ang = pos.astype(F32)[:, None] * inv[None, :]
    c, s = jnp.cos(ang), jnp.sin(ang)
    t = pos.shape[0]
    pad = jnp.zeros((t, LANES - QK_NOPE_DIM - QK_ROPE_DIM), F32)
    cos = jnp.concatenate([jnp.ones((t, QK_NOPE_DIM), F32), c, c, pad], axis=1)
    sin = jnp.concatenate([jnp.zeros((t, QK_NOPE_DIM), F32), -s, s, pad], axis=1)
    return cos, sin


def _pad_rows(a, axis, to):
    padw = [(0, 0)] * a.ndim
    padw[axis] = (0, to - a.shape[axis])
    return jnp.pad(a, padw)


PROJ_CW = 512


def _in_proj_body(u_ref, wz_ref, wx_ref, wg_ref, wm_ref, gb_ref, zs_ref, xbc_ref, gate_ref, misc_ref):
    x = u_ref[...]
    for c in range(0, D_INNER, PROJ_CW):
        cs = slice(c, c + PROJ_CW)
        z = _dot(x, wz_ref[:, cs])
        zs_ref[:, cs] = (z * jax.nn.sigmoid(z)).astype(zs_ref.dtype)
    for c in range(0, CONV_DIM, PROJ_CW):
        cs = slice(c, c + PROJ_CW)
        xbc_ref[:, cs] = _dot(x, wx_ref[:, cs])
    for c in range(0, 2 * D_MODEL, PROJ_CW):
        cs = slice(c, c + PROJ_CW)
        gate_ref[:, cs] = jax.nn.sigmoid(_dot(x, wg_ref[:, cs]) + gb_ref[:, cs]).astype(gate_ref.dtype)
    for c in range(0, MISC_W, PROJ_CW):
        cs = slice(c, c + PROJ_CW)
        misc_ref[:, cs] = _dot(x, wm_ref[:, cs])


def _in_proj(u, wts, *, tm_pref=512):
    m = u.shape[0]
    tm = _row_tile(m, tm_pref)
    widths = (D_INNER, CONV_DIM, 2 * D_MODEL, MISC_W)
    dtypes = (BF16, F32, BF16, F32)
    resident = lambda n: pl.BlockSpec((D_MODEL, n), lambda i: (0, 0), pipeline_mode=pl.Buffered(1))
    return pl.pallas_call(
        _in_proj_body,
        out_shape=[jax.ShapeDtypeStruct((m, n), dt) for n, dt in zip(widths, dtypes)],
        grid=(m // tm,),
        in_specs=[pl.BlockSpec((tm, D_MODEL), lambda i: (i, 0))] + [resident(n) for n in widths]
                 + [pl.BlockSpec((1, 2 * D_MODEL), lambda i: (0, 0))],
        out_specs=[pl.BlockSpec((tm, n), lambda i: (i, 0)) for n in widths],
        compiler_params=_cparams(1),
        name="in_proj",
    )(u, wts["w_z"], wts["w_xbc"], wts["w_gate"], wts["w_misc"], wts["gate_bias"])


def kernel(x_prompt, x_sample, cache_ckv, cache_krope, state_ssm, state_conv, meta_tokens, ffn1_norm, ffn1_w_gu, ffn1_w_down, mix_norm, w_in, gate_bias, conv_w, conv_b, dt_bias, a_log, ssm_d, ssm_norm, ssm_w_out, q_norm, w_uq, kv_norm, w_ukv, mla_w_out, w_mix_out, ffn2_norm, ffn2_w_gu, ffn2_w_down, final_norm):
    b, seq, _ = x_prompt.shape
    db, dseq, _ = x_sample.shape
    past = cache_ckv.shape[2]
    n_meta = meta_tokens.shape[0]
    nd = db * dseq
    wts = _prep_weights({
        "w_in": w_in[0], "w_uq": w_uq[0], "w_ukv": w_ukv[0], "mla_w_out": mla_w_out[0],
        "ssm_w_out": ssm_w_out[0], "w_mix_out": w_mix_out[0],
        "ffn1_w_gu": ffn1_w_gu[0], "ffn1_w_down": ffn1_w_down[0],
        "ffn2_w_gu": ffn2_w_gu[0], "ffn2_w_down": ffn2_w_down[0],
        "gate_bias": gate_bias[0], "conv_w": conv_w[0], "conv_b": conv_b[0], "dt_bias": dt_bias[0],
        "a_log": a_log[0], "ssm_d": ssm_d[0], "ssm_norm": ssm_norm[0],
        "q_norm": q_norm[0], "kv_norm": kv_norm[0]})

    xb = x_prompt.reshape(b * seq, D_MODEL)
    xs = jnp.concatenate([x_sample.reshape(nd, D_MODEL), meta_tokens.astype(F32)], axis=0)

    hb, ub = _ffn(xb, ffn1_norm[0], wts["ffn1_w_gu"], wts["ffn1_w_down"], mix_norm[0], emit_h=True, out_dtype=BF16)
    hs, us = _ffn(xs, ffn1_norm[0], wts["ffn1_w_gu"], wts["ffn1_w_down"], mix_norm[0], emit_h=True, out_dtype=BF16)
    zb, xbcb, gateb, miscb = _in_proj(ub, wts)
    zs, xbcs, gates, miscs = _in_proj(us, wts)

    L = SSD_L
    meta3 = lambda a: _pad_rows(a[nd:][None], 1, L)
    _, ssm_meta = _ssd(meta3(xbcs), meta3(miscs), meta3(zs),
                       jnp.zeros((1, CONV_WIDTH - 1, CONV_DIM), F32),
                       jnp.zeros((1, SSM_HEADS, SSM_HEAD_DIM, SSM_STATE), F32), wts, t_valid=n_meta)
    conv_meta = xbcs[nd + n_meta - (CONV_WIDTH - 1):nd + n_meta][None]
    yb, ssm_p = _ssd(xbcb.reshape(b, seq, -1), miscb.reshape(b, seq, -1), zb.reshape(b, seq, -1),
                     jnp.broadcast_to(conv_meta, (b,) + conv_meta.shape[1:]),
                     jnp.broadcast_to(ssm_meta, (b,) + ssm_meta.shape[1:]), wts, t_valid=None)
    dec3 = lambda a: _pad_rows(a[:nd].reshape(db, dseq, -1), 1, L)
    yd, ssm_s = _ssd(dec3(xbcs), dec3(miscs), dec3(zs), state_conv[0], state_ssm[0], wts, t_valid=dseq)
    yd = yd[:, :dseq].reshape(nd, D_INNER)
    conv_p = xbcb.reshape(b, seq, -1)[:, seq - (CONV_WIDTH - 1):]
    xbcd = xbcs[:nd].reshape(db, dseq, -1)
    conv_s = jnp.concatenate([state_conv[0].astype(F32), xbcd], axis=1)[:, -(CONV_WIDTH - 1):]

    tq = _row_tile(seq, 512)
    cos_b, sin_b = _rope_tables(jnp.arange(seq, dtype=jnp.int32))
    pos_s = jnp.concatenate([jnp.tile(past + jnp.arange(dseq, dtype=jnp.int32), db),
                             jnp.arange(n_meta, dtype=jnp.int32) - n_meta])
    cos_s, sin_s = _rope_tables(pos_s)
    qb, ckvb, krb = _mla_prep(miscb, cos_b, sin_b, seq // tq, wts, tm=tq)
    qs, ckvs, krs = _mla_prep(miscs, cos_s, sin_s, 1, wts, tm=miscs.shape[0])
    kb, vb = _kv_expand(ckvb, krb, wts, tm=tq)
    kp, vp = _kv_expand(_pad_rows(ckvs[nd:], 0, LANES), _pad_rows(krs[nd:], 0, LANES), wts, tm=LANES)
    tkd = -(-(past + dseq) // LANES) * LANES
    ckv_d = _pad_rows(jnp.concatenate([cache_ckv[0].astype(F32), ckvs[:nd].reshape(db, dseq, -1)], axis=1), 1, tkd)
    kr_cache = jnp.pad(cache_krope[0].astype(F32),
                       ((0, 0), (0, 0), (QK_NOPE_DIM, LANES - QK_NOPE_DIM - QK_ROPE_DIM)))
    kr_d = _pad_rows(jnp.concatenate([kr_cache, krs[:nd].reshape(db, dseq, -1)], axis=1), 1, tkd)
    kd, vd = _kv_expand(ckv_d.reshape(db * tkd, -1), kr_d.reshape(db * tkd, -1), wts,
                        tm=_row_tile(db * tkd, 512))

    hsplit = lambda a, nb: a.reshape(MLA_HEADS, nb, -1, LANES)
    ob = _attention(hsplit(qb, b), kb, hsplit(vb, b), kp, vp,
                    causal=True, tq=_row_tile(seq, ATTN_TQ), tk=_row_tile(seq, ATTN_TK), prefix_valid=n_meta)
    od = _attention(hsplit(qs[:, :nd], db), kd, hsplit(vd, db), kp, vp,
                    causal=False, tq=dseq, tk=tkd, prefix_valid=n_meta, kv_valid=past + dseq)

    h2b = _mix(hb, yb.reshape(b * seq, -1), ob.reshape(HEAD_SLABS, b * seq, SLAB_W), gateb, wts)
    h2d = _mix(hs[:nd], yd, od.reshape(HEAD_SLABS, nd, SLAB_W), gates[:nd], wts)
    (y_p,) = _ffn(h2b, ffn2_norm[0], wts["ffn2_w_gu"], wts["ffn2_w_down"], final_norm, emit_h=False, out_dtype=F32)
    (y_s,) = _ffn(h2d, ffn2_norm[0], wts["ffn2_w_gu"], wts["ffn2_w_down"], final_norm, emit_h=False, out_dtype=F32)

    kr32 = lambda a: a[..., QK_NOPE_DIM:QK_NOPE_DIM + QK_ROPE_DIM]
    ckv_meta = jnp.broadcast_to(ckvs[nd:][None], (b, n_meta, KV_LORA))
    kr_meta = jnp.broadcast_to(kr32(krs[nd:])[None], (b, n_meta, QK_ROPE_DIM))
    ckv_p = jnp.concatenate([ckv_meta, ckvb.reshape(b, seq, -1)], axis=1)
    kr_p = jnp.concatenate([kr_meta, kr32(krb).reshape(b, seq, -1)], axis=1)
    return (y_p.reshape(b, seq, D_MODEL), y_s.reshape(db, dseq, D_MODEL),
            ckv_p[None], kr_p[None], ssm_p[None], conv_p[None],
            ckvs[:nd].reshape(db, dseq, -1)[None], kr32(krs[:nd]).reshape(db, dseq, -1)[None],
            ssm_s[None], conv_s[None])
```
